```python
import math
import jax, jax.numpy as jnp
from jax import lax
import numpy as np

D_MODEL = 1024
BATCH = 2
SEQ = 16384
DEPTH = 4

GRID_W = 64
CTX_LEN = 256
HEAD_DIM = 64
ROPE_BASE = 10000.0
EPS = 1e-6

POOL_GROUPS = 4
POOL_GROUP_DIM = 64
POOL_WIDTH = POOL_GROUPS * POOL_GROUP_DIM
POOL_WINDOWS = (2, 4, 8, 16)
SWA_Q_HEADS = 12
SWA_KV_HEADS = 4
SWA_WINDOW = 128
SWA_BLOCK = 128
EVEN_IN = POOL_WIDTH + (SWA_Q_HEADS + 2 * SWA_KV_HEADS) * HEAD_DIM
EVEN_MIX = POOL_WIDTH + SWA_Q_HEADS * HEAD_DIM

HYENA_WIDTH = 512
HYENA_EMB = 33
HYENA_HIDDEN = 64
HYENA_FAST_DECAY = 0.3
HYENA_SLOW_DECAY = 1.5
HYENA_TARGET = 1e-2
MLA_HEADS = 8
MLA_NOPE = 64
MLA_ROPE = 32
MLA_QK = MLA_NOPE + MLA_ROPE
MLA_V = 64
MLA_Q_RANK = 256
MLA_KV_RANK = 128
ODD_IN = 3 * HYENA_WIDTH + MLA_Q_RANK + MLA_KV_RANK + MLA_ROPE
ODD_MIX = HYENA_WIDTH + MLA_HEADS * MLA_V
ATTN_BLOCK = 128

N_EXPERTS = 16
N_GROUPS = 4
EXPERTS_PER_GROUP = N_EXPERTS // N_GROUPS
TOP_K = 2
EXPERT_FF = 512

kernel_name = "hybrid_pool_swa_hyena_mla_moe_dit"


def rms(x, g):
    xf = x.astype(jnp.float32)
    y = xf * lax.rsqrt(jnp.mean(xf * xf, axis=-1, keepdims=True) + EPS)
    return (y * g.astype(jnp.float32)).astype(x.dtype)


def modulate(x, g, shift, scale):
    return rms(x, g) * (1 + scale) + shift


def axial_rope(n_lat, d_rot):
    rows = n_lat // GRID_W
    row = jnp.repeat(jnp.arange(rows), GRID_W).astype(jnp.float32)
    col = jnp.tile(jnp.arange(GRID_W), rows).astype(jnp.float32)
    n_freq = d_rot // 4
    inv = ROPE_BASE ** (-jnp.arange(n_freq, dtype=jnp.float32) / n_freq)
    ang = jnp.concatenate([row[:, None] * inv, col[:, None] * inv], axis=-1)
    return jnp.cos(ang), jnp.sin(ang)


def apply_rope(x, cos, sin):
    half = x.shape[-1] // 2
    x1, x2 = x[..., :half], x[..., half:]
    c = cos[:, None, :].astype(x.dtype)
    s = sin[:, None, :].astype(x.dtype)
    return jnp.concatenate([x1 * c - x2 * s, x2 * c + x1 * s], axis=-1)


def pool_mix(a, pool_w, pool_scale):
    B, L, _ = a.shape
    af = a.astype(jnp.float32)
    cs = jnp.concatenate([jnp.zeros_like(af[:, :1]), jnp.cumsum(af, axis=1)], axis=1)
    t = jnp.arange(L)
    outs = []
    for g, w in enumerate(POOL_WINDOWS):
        lo = jnp.clip(t - w // 2, 0, L)
        hi = jnp.clip(t + w // 2, 0, L)
        sl = slice(g * POOL_GROUP_DIM, (g + 1) * POOL_GROUP_DIM)
        cnt = (hi - lo).astype(jnp.float32)[None, :, None]
        outs.append((cs[:, hi, sl] - cs[:, lo, sl]) / cnt - af[:, :, sl])
    p = jnp.stack(outs, axis=2)
    y = jnp.einsum('blgc,gcd->blgd', p, pool_w.astype(jnp.float32)).reshape(B, L, POOL_WIDTH)
    return (y * pool_scale.astype(jnp.float32)).astype(a.dtype)


def swa_latent(q, k, v, kc, vc, sink):
    B, S = q.shape[:2]
    C = kc.shape[1]
    nb = S // SWA_BLOCK
    G = SWA_Q_HEADS // SWA_KV_HEADS
    nk = 3 * SWA_BLOCK
    qb = q.reshape(B, nb, SWA_BLOCK, SWA_KV_HEADS, G, HEAD_DIM)

    def band(t):
        tp = jnp.pad(t, ((0, 0), (SWA_BLOCK, SWA_BLOCK), (0, 0), (0, 0)))
        tp = tp.reshape(B, nb + 2, SWA_BLOCK, SWA_KV_HEADS, HEAD_DIM)
        return jnp.concatenate([tp[:, :nb], tp[:, 1:nb + 1], tp[:, 2:]], axis=2)

    kb, vb = band(k), band(v)
    blk = jnp.arange(nb)[:, None, None] * SWA_BLOCK
    qpos = blk + jnp.arange(SWA_BLOCK)[None, :, None]
    kpos = blk - SWA_BLOCK + jnp.arange(nk)[None, None, :]
    valid = (jnp.abs(kpos - qpos) <= SWA_WINDOW) & (kpos >= 0) & (kpos < S)
    scale = HEAD_DIM ** -0.5
    s_lat = jnp.einsum('bnqhgd,bnkhd->bhgnqk', qb, kb, preferred_element_type=jnp.float32) * scale
    s_lat = jnp.where(valid, s_lat, -jnp.inf)
    s_ctx = jnp.einsum('bnqhgd,bchd->bhgnqc', qb, kc, preferred_element_type=jnp.float32) * scale
    snk = jnp.broadcast_to(sink.astype(jnp.float32).reshape(SWA_KV_HEADS, G)[None, :, :, None, None, None],
                           s_lat.shape[:-1] + (1,))
    p = jax.nn.softmax(jnp.concatenate([s_lat, s_ctx, snk], axis=-1), axis=-1)
    o = (jnp.einsum('bhgnqk,bnkhd->bnqhgd', p[..., :nk].astype(vb.dtype), vb)
         + jnp.einsum('bhgnqc,bchd->bnqhgd', p[..., nk:nk + C].astype(vc.dtype), vc))
    return o.reshape(B, S, SWA_Q_HEADS * HEAD_DIM)


def swa_context(qc, kc, vc, sink):
    B, C = qc.shape[:2]
    G = SWA_Q_HEADS // SWA_KV_HEADS
    q = qc.reshape(B, C, SWA_KV_HEADS, G, HEAD_DIM)
    s = jnp.einsum('bqhgd,bkhd->bhgqk', q, kc, preferred_element_type=jnp.float32) * HEAD_DIM ** -0.5
    snk = jnp.broadcast_to(sink.astype(jnp.float32).reshape(SWA_KV_HEADS, G)[None, :, :, None, None],
                           s.shape[:-1] + (1,))
    p = jax.nn.softmax(jnp.concatenate([s, snk], axis=-1), axis=-1)[..., :C]
    o = jnp.einsum('bhgqk,bkhd->bqhgd', p.astype(vc.dtype), vc)
    return o.reshape(B, C, SWA_Q_HEADS * HEAD_DIM)


def split_even(p):
    B, L = p.shape[:2]
    o1 = POOL_WIDTH
    o2 = o1 + SWA_Q_HEADS * HEAD_DIM
    o3 = o2 + SWA_KV_HEADS * HEAD_DIM
    a = p[..., :o1]
    q = p[..., o1:o2].reshape(B, L, SWA_Q_HEADS, HEAD_DIM)
    k = p[..., o2:o3].reshape(B, L, SWA_KV_HEADS, HEAD_DIM)
    v = p[..., o3:].reshape(B, L, SWA_KV_HEADS, HEAD_DIM)
    return a, q, k, v


def even_mixer(hl, hc, w_in, w_out, pool_w, pool_scale, q_gain, k_gain, sink, rope, need_ctx):
    al, ql, kl, vl = split_even(hl @ w_in)
    ac, qc, kc, vc = split_even(hc @ w_in)
    ql = apply_rope(rms(ql, q_gain), *rope)
    kl = apply_rope(rms(kl, k_gain), *rope)
    kc = rms(kc, k_gain)
    out_l = jnp.concatenate([pool_mix(al, pool_w, pool_scale),
                             swa_latent(ql, kl, vl, kc, vc, sink)], axis=-1) @ w_out
    out_c = None
    if need_ctx:
        qc = rms(qc, q_gain)
        out_c = jnp.concatenate([pool_mix(ac, pool_w, pool_scale),
                                 swa_context(qc, kc, vc, sink)], axis=-1) @ w_out
    return out_l, out_c


def short_conv(x, w, b):
    L = x.shape[1]
    xp = jnp.pad(x, ((0, 0), (1, 1), (0, 0)))
    return xp[:, :L] * w[0] + xp[:, 1:L + 1] * w[1] + xp[:, 2:] * w[2] + b


def hyena_filter(L, w1, b1, w2, b2, w3, freq):
    f32 = jnp.float32
    pos = jnp.arange(L, dtype=f32)
    t = pos / max(L - 1, 1)
    bands = (HYENA_EMB - 1) // 2
    fr = jnp.linspace(1e-4, bands - 1, bands, dtype=f32)
    ang = (2 * math.pi / L) * pos[:, None] * fr
    z = jnp.concatenate([t[:, None], jnp.cos(ang), -jnp.sin(ang)], axis=-1)
    h = jnp.sin(freq[0].astype(f32) * (z @ w1.astype(f32) + b1.astype(f32)))
    h = jnp.sin(freq[1].astype(f32) * (h @ w2.astype(f32) + b2.astype(f32)))
    h = h @ w3.astype(f32)
    deltas = jnp.abs(jnp.linspace(math.log(HYENA_TARGET) / HYENA_FAST_DECAY,
                                  math.log(HYENA_TARGET) / HYENA_SLOW_DECAY, HYENA_WIDTH, dtype=f32))
    decay = jnp.exp(-t[:, None] * deltas)
    h_fwd = h[:, :HYENA_WIDTH] * decay
    h_bwd = h[:, HYENA_WIDTH:] * decay
    g = jnp.concatenate([h_fwd, jnp.zeros_like(h_fwd[:1]), h_bwd[:0:-1]], axis=0)
    return g / jnp.sum(jnp.abs(g), axis=0, keepdims=True)


def hyena_mix(u, conv_w, conv_b, w1, b1, w2, b2, w3, freq, skip):
    L = u.shape[1]
    uc = short_conv(u, conv_w, conv_b).astype(jnp.float32)
    x0 = uc[..., :HYENA_WIDTH]
    x1 = uc[..., HYENA_WIDTH:2 * HYENA_WIDTH]
    v = uc[..., 2 * HYENA_WIDTH:]
    z = v * x1
    g = hyena_filter(L, w1, b1, w2, b2, w3, freq)
    y = jnp.fft.irfft(jnp.fft.rfft(z, n=2 * L, axis=1) * jnp.fft.rfft(g, n=2 * L, axis=0)[None],
                      n=2 * L, axis=1)[:, :L]
    return (x0 * (y + skip.astype(jnp.float32) * z)).astype(u.dtype)


def mla_queries(cq, cq_gain, w_uq, q_gain, rope):
    B, L = cq.shape[:2]
    q = (rms(cq, cq_gain) @ w_uq).reshape(B, L, MLA_HEADS, MLA_QK)
    q = rms(q, q_gain)
    if rope is not None:
        q = jnp.concatenate([q[..., :MLA_NOPE], apply_rope(q[..., MLA_NOPE:], *rope)], axis=-1)
    return q


def mla_keys_values(ckv, kr, ckv_gain, w_ukv, k_gain, rope):
    B, L = ckv.shape[:2]
    kv = (rms(ckv, ckv_gain) @ w_ukv).reshape(B, L, MLA_HEADS, MLA_NOPE + MLA_V)
    k = jnp.concatenate([kv[..., :MLA_NOPE],
                         jnp.broadcast_to(kr[:, :, None, :], (B, L, MLA_HEADS, MLA_ROPE))], axis=-1)
    k = rms(k, k_gain)
    if rope is not None:
        k = jnp.concatenate([k[..., :MLA_NOPE], apply_rope(k[..., MLA_NOPE:], *rope)], axis=-1)
    return k, kv[..., MLA_NOPE:]


def mla_latent(ql, kl, vl, kc, vc):
    B, S = ql.shape[:2]
    nb = S // ATTN_BLOCK
    k_all = jnp.concatenate([kl, kc], axis=1)
    v_all = jnp.concatenate([vl, vc], axis=1)
    qb = ql.reshape(B, nb, ATTN_BLOCK, MLA_HEADS, MLA_QK).transpose(1, 0, 2, 3, 4)
    scale = MLA_QK ** -0.5

    def one_block(q):
        s = jnp.einsum('bqhd,bkhd->bhqk', q, k_all, preferred_element_type=jnp.float32) * scale
        p = jax.nn.softmax(s, axis=-1)
        return jnp.einsum('bhqk,bkhd->bqhd', p.astype(v_all.dtype), v_all)

    o = lax.map(one_block, qb)
    return o.transpose(1, 0, 2, 3, 4).reshape(B, S, MLA_HEADS * MLA_V)


def mla_context(qc, kc, vc):
    B, C = qc.shape[:2]
    s = jnp.einsum('bqhd,bkhd->bhqk', qc, kc, preferred_element_type=jnp.float32) * MLA_QK ** -0.5
    p = jax.nn.softmax(s, axis=-1)
    return jnp.einsum('bhqk,bkhd->bqhd', p.astype(vc.dtype), vc).reshape(B, C, MLA_HEADS * MLA_V)


def split_odd(p):
    o1 = 3 * HYENA_WIDTH
    o2 = o1 + MLA_Q_RANK
    o3 = o2 + MLA_KV_RANK
    return p[..., :o1], p[..., o1:o2], p[..., o2:o3], p[..., o3:]


def odd_mixer(hl, hc, w_in, w_out, conv_w, conv_b, w1, b1, w2, b2, w3, freq, skip,
              cq_gain, ckv_gain, w_uq, w_ukv, q_gain, k_gain, rope, need_ctx):
    ul, cql, ckvl, krl = split_odd(hl @ w_in)
    uc, cqc, ckvc, krc = split_odd(hc @ w_in)
    ql = mla_queries(cql, cq_gain, w_uq, q_gain, rope)
    kl, vl = mla_keys_values(ckvl, krl, ckv_gain, w_ukv, k_gain, rope)
    kc, vc = mla_keys_values(ckvc, krc, ckv_gain, w_ukv, k_gain, None)
    out_l = jnp.concatenate([hyena_mix(ul, conv_w, conv_b, w1, b1, w2, b2, w3, freq, skip),
                             mla_latent(ql, kl, vl, kc, vc)], axis=-1) @ w_out
    out_c = None
    if need_ctx:
        qc = mla_queries(cqc, cq_gain, w_uq, q_gain, None)
        out_c = jnp.concatenate([hyena_mix(uc, conv_w, conv_b, w1, b1, w2, b2, w3, freq, skip),
                                 mla_context(qc, kc, vc)], axis=-1) @ w_out
    return out_l, out_c


def moe(h, router_w, router_b, w_gate, w_up, w_down):
    T = h.shape[0]
    scores = jax.nn.sigmoid(jnp.dot(h, router_w, preferred_element_type=jnp.float32))
    biased = scores + router_b.astype(jnp.float32)
    grp_score = lax.top_k(biased.reshape(T, N_GROUPS, EXPERTS_PER_GROUP), 2)[0].sum(-1)
    sel_group = jnp.argmax(grp_score, axis=-1)
    in_group = (jnp.arange(N_EXPERTS) // EXPERTS_PER_GROUP)[None, :] == sel_group[:, None]
    _, idx = lax.top_k(jnp.where(in_group, biased, -jnp.inf), TOP_K)
    w_sel = jnp.take_along_axis(scores, idx, axis=-1)
    w_sel = w_sel / jnp.sum(w_sel, axis=-1, keepdims=True)
    combine = jnp.sum(jax.nn.one_hot(idx, N_EXPERTS, dtype=jnp.float32) * w_sel[..., None], axis=1)
    out = jnp.zeros(h.shape, jnp.float32)
    for e in range(N_EXPERTS):
        a = jax.nn.silu(h @ w_gate[e]) * (h @ w_up[e])
        out = out + combine[:, e:e + 1] * (a @ w_down[e])
    return out.astype(h.dtype)


def setup_inputs(seed: int = 0) -> dict:
    key = jax.random.key(seed)
    ks = iter(jax.random.split(key, 48))
    f32 = jnp.float32
    n_even = (DEPTH + 1) // 2
    n_odd = DEPTH // 2

    def nrm(shape, scale):
        return jax.random.normal(next(ks), shape, f32) * scale

    def gain(shape):
        return 1.0 + nrm(shape, 0.05)

    D = D_MODEL
    return {
        "x": nrm((BATCH, SEQ, D), 1.0),
        "c": nrm((BATCH, D), 1.0),
        "ctx": nrm((BATCH, CTX_LEN, D), 1.0),
        "c_ctx": nrm((D,), 1.0),
        "ada_w": nrm((DEPTH, D, 6 * D), 0.5 * D ** -0.5),
        "ada_b": nrm((DEPTH, 6 * D), 0.01),
        "norm_g": gain((DEPTH, 2, D)),
        "ev_w_in": nrm((n_even, D, EVEN_IN), D ** -0.5),
        "ev_w_out": nrm((n_even, EVEN_MIX, D), EVEN_MIX ** -0.5),
        "pool_w": nrm((n_even, POOL_GROUPS, POOL_GROUP_DIM, POOL_GROUP_DIM), POOL_GROUP_DIM ** -0.5),
        "pool_scale": 1.0 + nrm((n_even, POOL_WIDTH), 0.1),
        "swa_q_gain": gain((n_even, HEAD_DIM)),
        "swa_k_gain": gain((n_even, HEAD_DIM)),
        "swa_sink": nrm((n_even, SWA_Q_HEADS), 1.0),
        "od_w_in": nrm((n_odd, D, ODD_IN), D ** -0.5),
        "od_w_out": nrm((n_odd, ODD_MIX, D), ODD_MIX ** -0.5),
        "hy_conv_w": nrm((n_odd, 3, 3 * HYENA_WIDTH), 0.5),
        "hy_conv_b": nrm((n_odd, 3 * HYENA_WIDTH), 0.01),
        "hy_w1": nrm((n_odd, HYENA_EMB, HYENA_HIDDEN), HYENA_EMB ** -0.5),
        "hy_b1": nrm((n_odd, HYENA_HIDDEN), 0.1),
        "hy_w2": nrm((n_odd, HYENA_HIDDEN, HYENA_HIDDEN), HYENA_HIDDEN ** -0.5),
        "hy_b2": nrm((n_odd, HYENA_HIDDEN), 0.1),
        "hy_w3": nrm((n_odd, HYENA_HIDDEN, 2 * HYENA_WIDTH), HYENA_HIDDEN ** -0.5),
        "hy_freq": gain((n_odd, 2, HYENA_HIDDEN)),
        "hy_skip": nrm((n_odd, HYENA_WIDTH), 1.0),
        "mla_cq_gain": gain((n_odd, MLA_Q_RANK)),
        "mla_ckv_gain": gain((n_odd, MLA_KV_RANK)),
        "mla_w_uq": nrm((n_odd, MLA_Q_RANK, MLA_HEADS * MLA_QK), MLA_Q_RANK ** -0.5),
        "mla_w_ukv": nrm((n_odd, MLA_KV_RANK, MLA_HEADS * (MLA_NOPE + MLA_V)), MLA_KV_RANK ** -0.5),
        "mla_q_gain": gain((n_odd, MLA_QK)),
        "mla_k_gain": gain((n_odd, MLA_QK)),
        "router_w": nrm((D, N_EXPERTS), D ** -0.5),
        "router_b": nrm((N_EXPERTS,), 0.01),
        "moe_w_gate": nrm((DEPTH, N_EXPERTS, D, EXPERT_FF), D ** -0.5),
        "moe_w_up": nrm((DEPTH, N_EXPERTS, D, EXPERT_FF), D ** -0.5),
        "moe_w_down": nrm((DEPTH, N_EXPERTS, EXPERT_FF, D), EXPERT_FF ** -0.5),
    }


def reference(x, c, ctx, c_ctx, ada_w, ada_b, norm_g, ev_w_in, ev_w_out, pool_w, pool_scale,
              swa_q_gain, swa_k_gain, swa_sink, od_w_in, od_w_out, hy_conv_w, hy_conv_b,
              hy_w1, hy_b1, hy_w2, hy_b2, hy_w3, hy_freq, hy_skip, mla_cq_gain, mla_ckv_gain,
              mla_w_uq, mla_w_ukv, mla_q_gain, mla_k_gain, router_w, router_b,
              moe_w_gate, moe_w_up, moe_w_down):
    B, S, D = x.shape
    C = ctx.shape[1]
    rope_swa = axial_rope(S, HEAD_DIM)
    rope_mla = axial_rope(S, MLA_ROPE)
    silu_c = jax.nn.silu(c)
    silu_cc = jax.nn.silu(c_ctx)
    xl, xc = x, ctx
    for layer in range(DEPTH):
        need_ctx = layer < DEPTH - 1
        mod_l = (silu_c @ ada_w[layer] + ada_b[layer])[:, None, :]
        mod_c = silu_cc @ ada_w[layer] + ada_b[layer]
        sh1, sc1, g1, sh2, sc2, g2 = jnp.split(mod_l, 6, axis=-1)
        csh1, csc1, cg1, csh2, csc2, cg2 = jnp.split(mod_c, 6, axis=-1)
        hl = modulate(xl, norm_g[layer, 0], sh1, sc1)
        hc = modulate(xc, norm_g[layer, 0], csh1, csc1)
        if layer % 2 == 0:
            i = layer // 2
            ml, mc = even_mixer(hl, hc, ev_w_in[i], ev_w_out[i], pool_w[i], pool_scale[i],
                                swa_q_gain[i], swa_k_gain[i], swa_sink[i], rope_swa, need_ctx)
        else:
            i = layer // 2
            ml, mc = odd_mixer(hl, hc, od_w_in[i], od_w_out[i], hy_conv_w[i], hy_conv_b[i],
                               hy_w1[i], hy_b1[i], hy_w2[i], hy_b2[i], hy_w3[i], hy_freq[i], hy_skip[i],
                               mla_cq_gain[i], mla_ckv_gain[i], mla_w_uq[i], mla_w_ukv[i],
                               mla_q_gain[i], mla_k_gain[i], rope_mla, need_ctx)
        xl = xl + g1 * ml
        hl2 = modulate(xl, norm_g[layer, 1], sh2, sc2).reshape(B * S, D)
        if need_ctx:
            xc = xc + cg1 * mc
            hc2 = modulate(xc, norm_g[layer, 1], csh2, csc2).reshape(B * C, D)
            y = moe(jnp.concatenate([hl2, hc2], axis=0), router_w, router_b,
                    moe_w_gate[layer], moe_w_up[layer], moe_w_down[layer])
            xc = xc + cg2 * y[B * S:].reshape(B, C, D)
            yl = y[:B * S]
        else:
            yl = moe(hl2, router_w, router_b, moe_w_gate[layer], moe_w_up[layer], moe_w_down[layer])
        xl = xl + g2 * yl.reshape(B, S, D)
    return xl
```

```python
import functools
import math

import jax
import jax.numpy as jnp
from jax import lax
from jax.experimental import pallas as pl
from jax.experimental.pallas import tpu as pltpu

F32 = jnp.float32
BF16 = jnp.bfloat16

GRID_W = 64
HEAD_DIM = 64
ROPE_BASE = 10000.0
EPS = 1e-6
POOL_GROUPS = 4
POOL_GROUP_DIM = 64
POOL_WIDTH = POOL_GROUPS * POOL_GROUP_DIM
POOL_WINDOWS = (2, 4, 8, 16)
SWA_Q_HEADS = 12
SWA_KV_HEADS = 4
SWA_GROUP = SWA_Q_HEADS // SWA_KV_HEADS
SWA_WINDOW = 128
SWA_BLOCK = 128
HYENA_WIDTH = 512
HYENA_EMB = 33
HYENA_HIDDEN = 64
HYENA_FAST_DECAY = 0.3
HYENA_SLOW_DECAY = 1.5
HYENA_TARGET = 1e-2
MLA_HEADS = 8
MLA_NOPE = 64
MLA_ROPE = 32
MLA_QK = MLA_NOPE + MLA_ROPE
MLA_V = 64
MLA_Q_RANK = 256
MLA_KV_RANK = 128
N_EXPERTS = 16
N_GROUPS = 4
EXPERTS_PER_GROUP = N_EXPERTS // N_GROUPS
EXPERT_FF = 512

LANES = 128
TM = 256
DFT_N2 = 128
VMEM_LIMIT = 56 * 1024 * 1024

NT_DIMS = (((1,), (1,)), ((), ()))
TN_DIMS = (((0,), (0,)), ((), ()))


def _dot(a, b, dims=None, precision=None):
    if dims is None:
        return jnp.dot(a, b, preferred_element_type=F32, precision=precision)
    return lax.dot_general(a, b, dims, preferred_element_type=F32, precision=precision)


def _params(sem):
    return pltpu.CompilerParams(dimension_semantics=sem, vmem_limit_bytes=VMEM_LIMIT)


def _mod_row(ntb):
    def f(i):
        return jnp.where(i % ntb == ntb - 1, 2, i // ntb)
    return f


def _modulate(x, g, shift, scale):
    ms = jnp.mean(x * x, axis=-1, keepdims=True)
    return (x * lax.rsqrt(ms + EPS) * g) * (1.0 + scale) + shift


def _head_norm_rope(xh, real_dim, gain, cos, sa, sb, half):
    r = lax.rsqrt(jnp.sum(xh * xh, axis=-1, keepdims=True) * (1.0 / real_dim) + EPS)
    xn = xh * r * gain
    return xn * cos + pltpu.roll(xn, half, 1) * sa + pltpu.roll(xn, LANES - half, 1) * sb


def _adaln_kernel(c_ref, w_ref, b_ref, o_ref):
    c = c_ref[...]
    s = (c * jax.nn.sigmoid(c)).astype(BF16)
    o_ref[...] = _dot(s, w_ref[...].astype(BF16)) + b_ref[...]


def _adaln(cvec, ada_w, ada_b):
    depth, d, d6 = ada_w.shape
    nchunk = d6 // d
    out = pl.pallas_call(
        _adaln_kernel,
        grid=(depth, nchunk),
        in_specs=[pl.BlockSpec((8, d), lambda l, j: (0, 0)),
                  pl.BlockSpec((None, d, d), lambda l, j: (l, 0, j)),
                  pl.BlockSpec((None, 1, d), lambda l, j: (l, 0, j))],
        out_specs=pl.BlockSpec((None, None, 8, d), lambda l, j: (l, j, 0, 0)),
        out_shape=jax.ShapeDtypeStruct((depth, nchunk, 8, d), F32),
        compiler_params=_params(("parallel", "parallel")),
        name="adaln",
    )(cvec, ada_w, ada_b.reshape(depth, 1, d6))
    mod = jnp.transpose(out, (0, 2, 1, 3))
    return jnp.pad(mod, ((0, 0), (0, 0), (0, 8 - nchunk), (0, 0)))


def _even_in_kernel(*refs, has_prev):
    if has_prev:
        (x_ref, y_ref, pmod_ref, mod_ref, g_ref, w_ref, qg_ref, kg_ref, cos_ref, sa_ref, sb_ref,
         xo_ref, a_ref, q_ref, k_ref, v_ref) = refs
        x = x_ref[...] + pmod_ref[5:6, :] * y_ref[...]
        xo_ref[...] = x
    else:
        (x_ref, mod_ref, g_ref, w_ref, qg_ref, kg_ref, cos_ref, sa_ref, sb_ref,
         a_ref, q_ref, k_ref, v_ref) = refs
        x = x_ref[...]
    h = _modulate(x, g_ref[...], mod_ref[0:1, :], mod_ref[1:2, :]).astype(BF16)
    p = _dot(h, w_ref[...])
    a_ref[...] = p[:, :POOL_WIDTH]
    cos, sa, sb = cos_ref[...], sa_ref[...], sb_ref[...]
    o = POOL_WIDTH
    for hh in range(SWA_Q_HEADS):
        xh = p[:, o + LANES * hh:o + LANES * (hh + 1)]
        q_ref[:, LANES * hh:LANES * (hh + 1)] = _head_norm_rope(
            xh, HEAD_DIM, qg_ref[...], cos, sa, sb, HEAD_DIM // 2).astype(BF16)
    o += SWA_Q_HEADS * LANES
    for hh in range(SWA_KV_HEADS):
        xh = p[:, o + LANES * hh:o + LANES * (hh + 1)]
        k_ref[:, LANES * hh:LANES * (hh + 1)] = _head_norm_rope(
            xh, HEAD_DIM, kg_ref[...], cos, sa, sb, HEAD_DIM // 2).astype(BF16)
    o += SWA_KV_HEADS * LANES
    ones_hi = (lax.broadcasted_iota(jnp.int32, (1, LANES), 1) >= HEAD_DIM).astype(F32)
    for hh in range(SWA_KV_HEADS):
        vh = p[:, o + LANES * hh:o + LANES * (hh + 1)]
        v_ref[:, LANES * hh:LANES * (hh + 1)] = (vh + ones_hi).astype(BF16)


def _odd_in_kernel(x_ref, y_ref, pmod_ref, mod_ref, g_ref, w_ref, cqg_ref, ckvg_ref, wuq_ref, wuk_ref,
                   wuv_ref, qg_ref, kg_ref, cos_ref, sa_ref, sb_ref,
                   xo_ref, u_ref, q_ref, k_ref, v_ref):
    x = x_ref[...] + pmod_ref[5:6, :] * y_ref[...]
    xo_ref[...] = x
    h = _modulate(x, g_ref[...], mod_ref[0:1, :], mod_ref[1:2, :]).astype(BF16)
    p = _dot(h, w_ref[...])
    nu = 3 * HYENA_WIDTH
    u_ref[...] = p[:, :nu]
    cq = p[:, nu:nu + MLA_Q_RANK]
    ckv = p[:, nu + MLA_Q_RANK:nu + MLA_Q_RANK + MLA_KV_RANK]
    krb = p[:, nu + MLA_Q_RANK + MLA_KV_RANK:]
    cqn = (cq * lax.rsqrt(jnp.mean(cq * cq, axis=-1, keepdims=True) + EPS) * cqg_ref[...]).astype(BF16)
    ckvn = (ckv * lax.rsqrt(jnp.mean(ckv * ckv, axis=-1, keepdims=True) + EPS) * ckvg_ref[...]).astype(BF16)
    qp = _dot(cqn, wuq_ref[...])
    kp = _dot(ckvn, wuk_ref[...])
    vp = _dot(ckvn, wuv_ref[...])
    krp = pltpu.roll(krb, MLA_NOPE, 1)
    cos, sa, sb = cos_ref[...], sa_ref[...], sb_ref[...]
    ones_hi = (lax.broadcasted_iota(jnp.int32, (1, LANES), 1) >= MLA_V).astype(F32)
    for hh in range(MLA_HEADS):
        sl = slice(LANES * hh, LANES * (hh + 1))
        q_ref[:, sl] = _head_norm_rope(qp[:, sl], MLA_QK, qg_ref[...], cos, sa, sb,
                                       MLA_ROPE // 2).astype(BF16)
        k_ref[:, sl] = _head_norm_rope(kp[:, sl] + krp, MLA_QK, kg_ref[...], cos, sa, sb,
                                       MLA_ROPE // 2).astype(BF16)
        v_ref[:, sl] = (vp[:, sl] + ones_hi).astype(BF16)


def _tok_spec(width):
    return pl.BlockSpec((TM, width), lambda i: (i, 0))


def _full_spec(shape):
    nd = len(shape)
    return pl.BlockSpec(shape, lambda i: (0,) * nd)


def _even_in(x, prev, mod, g, w, qg, kg, tabs, ntb):
    rows, d = x.shape
    row = _mod_row(ntb)
    mod_spec = pl.BlockSpec((None, 8, d), lambda i: (row(i), 0, 0))
    tab_spec = pl.BlockSpec((TM, LANES), lambda i: (i % ntb, 0))
    nq, nk = SWA_Q_HEADS * LANES, SWA_KV_HEADS * LANES
    ins = [x]
    specs = [_tok_spec(d)]
    outs = []
    ospecs = []
    if prev is not None:
        y, pmod = prev
        ins += [y, pmod]
        specs += [_tok_spec(d), mod_spec]
        outs.append(jax.ShapeDtypeStruct((rows, d), F32))
        ospecs.append(_tok_spec(d))
    ins += [mod, g, w, qg, kg, *tabs]
    specs += [mod_spec, _full_spec(g.shape), _full_spec(w.shape), _full_spec(qg.shape), _full_spec(kg.shape),
              tab_spec, tab_spec, tab_spec]
    outs += [jax.ShapeDtypeStruct((rows, POOL_WIDTH), F32), jax.ShapeDtypeStruct((rows, nq), BF16),
             jax.ShapeDtypeStruct((rows, nk), BF16), jax.ShapeDtypeStruct((rows, nk), BF16)]
    ospecs += [_tok_spec(POOL_WIDTH), _tok_spec(nq), _tok_spec(nk), _tok_spec(nk)]
    res = pl.pallas_call(
        functools.partial(_even_in_kernel, has_prev=prev is not None),
        grid=(rows // TM,), in_specs=specs, out_specs=ospecs, out_shape=outs,
        compiler_params=_params(("parallel",)), name="even_in",
    )(*ins)
    if prev is None:
        return (x, *res)
    return res


def _odd_in(x, prev, mod, g, w, cqg, ckvg, wuq, wuk, wuv, qg, kg, tabs, ntb):
    rows, d = x.shape
    row = _mod_row(ntb)
    mod_spec = pl.BlockSpec((None, 8, d), lambda i: (row(i), 0, 0))
    tab_spec = pl.BlockSpec((TM, LANES), lambda i: (i % ntb, 0))
    y, pmod = prev
    nh = MLA_HEADS * LANES
    consts = [g, w, cqg, ckvg, wuq, wuk, wuv, qg, kg]
    return pl.pallas_call(
        _odd_in_kernel,
        grid=(rows // TM,),
        in_specs=[_tok_spec(d), _tok_spec(d), mod_spec, mod_spec] + [_full_spec(c.shape) for c in consts]
        + [tab_spec, tab_spec, tab_spec],
        out_specs=[_tok_spec(d), _tok_spec(3 * HYENA_WIDTH), _tok_spec(nh), _tok_spec(nh), _tok_spec(nh)],
        out_shape=[jax.ShapeDtypeStruct((rows, d), F32), jax.ShapeDtypeStruct((rows, 3 * HYENA_WIDTH), F32),
                   jax.ShapeDtypeStruct((rows, nh), BF16), jax.ShapeDtypeStruct((rows, nh), BF16),
                   jax.ShapeDtypeStruct((rows, nh), BF16)],
        compiler_params=_params(("parallel",)), name="odd_in",
    )(x, y, pmod, mod, *consts, *tabs)


def _halo_specs(width, ntb):
    per = TM // 8

    def prev_map(i):
        return (jnp.maximum(i * per - 1, 0), 0)

    def next_map(i):
        return ((i + 1) * per - jnp.where(i % ntb == ntb - 1, 1, 0), 0)
    return pl.BlockSpec((8, width), prev_map), pl.BlockSpec((8, width), next_map)


def _halo_valid(i, ntb):
    j = i % ntb
    prev_ok = jnp.logical_and(j != 0, j != ntb - 1)
    next_ok = j < ntb - 2
    return prev_ok, next_ok


def _pool_kernel(a_ref, ap_ref, an_ref, w_ref, sc_ref, o_ref, *, ntb, seq, ctx):
    i = pl.program_id(0)
    prev_ok, next_ok = _halo_valid(i, ntb)
    a = a_ref[...]
    ap = jnp.where(prev_ok, ap_ref[...], 0.0)
    an = jnp.where(next_ok, an_ref[...], 0.0)
    ext = jnp.concatenate([ap, a, an], axis=0)
    rows_ext = TM + 16

    def shifted(d):
        return pltpu.roll(ext, (-d) % rows_ext, 0)[8:8 + TM]

    j = i % ntb
    is_ctx = j == ntb - 1
    pos = jnp.where(is_ctx, 0, j * TM) + lax.broadcasted_iota(jnp.int32, (TM, 1), 0)
    length = jnp.where(is_ctx, ctx, seq)
    lane = lax.broadcasted_iota(jnp.int32, (1, POOL_WIDTH), 1)
    acc = a
    lo, hi = 0, 1
    pooled = jnp.zeros_like(a)
    for g, w in enumerate(POOL_WINDOWS):
        for d in list(range(-w // 2, lo)) + list(range(hi, w // 2)):
            acc = acc + shifted(d)
        lo, hi = -w // 2, w // 2
        cnt = (jnp.minimum(pos + w // 2, length) - jnp.maximum(pos - w // 2, 0)).astype(F32)
        pg = acc / cnt - a
        in_group = jnp.logical_and(lane >= g * POOL_GROUP_DIM, lane < (g + 1) * POOL_GROUP_DIM)
        pooled = jnp.where(in_group, pg, pooled)
    y = _dot(pooled.astype(BF16), w_ref[...]) * sc_ref[...]
    o_ref[...] = y.astype(BF16)


def _pool(a, w_bd, scale, ntb, seq, ctx):
    rows = a.shape[0]
    prev_spec, next_spec = _halo_specs(POOL_WIDTH, ntb)
    return pl.pallas_call(
        functools.partial(_pool_kernel, ntb=ntb, seq=seq, ctx=ctx),
        grid=(rows // TM,),
        in_specs=[_tok_spec(POOL_WIDTH), prev_spec, next_spec, _full_spec(w_bd.shape), _full_spec(scale.shape)],
        out_specs=_tok_spec(POOL_WIDTH),
        out_shape=jax.ShapeDtypeStruct((rows, POOL_WIDTH), BF16),
        compiler_params=_params(("parallel",)), name="pool",
    )(a, a, a, w_bd, scale)


def _finish_heads(acc, extra_den):
    lane = lax.broadcasted_iota(jnp.int32, (1, LANES), 1)
    den = jnp.where(lane < HEAD_DIM, pltpu.roll(acc, HEAD_DIM, 1) + extra_den, 1.0)
    return acc / den


def _sink_col(sink_ref, g, nrow):
    r = lax.broadcasted_iota(jnp.int32, (SWA_GROUP * nrow, 1), 0)
    col = jnp.zeros((SWA_GROUP * nrow, 1), F32)
    for t in range(SWA_GROUP):
        col = jnp.where(r // nrow == t, sink_ref[g * SWA_GROUP + t], col)
    return col


def _swa_lat_kernel(sink_ref, q_ref, kp_ref, kc_ref, kn_ref, kx_ref, vp_ref, vc_ref, vn_ref, vx_ref, o_ref,
                    *, nblk):
    g = pl.program_id(1)
    i = pl.program_id(2)
    q = jnp.concatenate([q_ref[:, LANES * t:LANES * (t + 1)] for t in range(SWA_GROUP)], axis=0)
    kb = jnp.concatenate([kp_ref[...], kc_ref[...], kn_ref[...]], axis=0)
    vb = jnp.concatenate([vp_ref[...], vc_ref[...], vn_ref[...]], axis=0)
    s_b = _dot(q, kb, NT_DIMS)
    s_c = _dot(q, kx_ref[...], NT_DIMS)
    nq = SWA_GROUP * SWA_BLOCK
    qrow = lax.broadcasted_iota(jnp.int32, (nq, 1), 0) % SWA_BLOCK
    kcol = lax.broadcasted_iota(jnp.int32, (1, 3 * SWA_BLOCK), 1)
    valid = jnp.logical_and(kcol >= qrow, kcol <= qrow + 2 * SWA_WINDOW)
    blk = kcol // SWA_BLOCK
    valid = jnp.logical_and(valid, jnp.logical_or(blk != 0, i > 0))
    valid = jnp.logical_and(valid, jnp.logical_or(blk != 2, i < nblk - 1))
    s_b = jnp.where(valid, s_b, -jnp.inf)
    sink = _sink_col(sink_ref, g, SWA_BLOCK)
    m = jnp.maximum(jnp.maximum(jnp.max(s_b, axis=-1, keepdims=True), jnp.max(s_c, axis=-1, keepdims=True)),
                    sink)
    p_b = jnp.exp(s_b - m).astype(BF16)
    p_c = jnp.exp(s_c - m).astype(BF16)
    acc = _dot(p_b, vb) + _dot(p_c, vx_ref[...])
    o = _finish_heads(acc, jnp.exp(sink - m)).astype(BF16)
    for t in range(SWA_GROUP):
        o_ref[:, LANES * t:LANES * (t + 1)] = o[SWA_BLOCK * t:SWA_BLOCK * (t + 1)]


def _swa_ctx_kernel(sink_ref, q_ref, kx_ref, vx_ref, prev_ref, o_ref, *, ctx):
    del prev_ref
    g = pl.program_id(1)
    q = jnp.concatenate([q_ref[:, LANES * t:LANES * (t + 1)] for t in range(SWA_GROUP)], axis=0)
    s = _dot(q, kx_ref[...], NT_DIMS)
    sink = _sink_col(sink_ref, g, ctx)
    m = jnp.maximum(jnp.max(s, axis=-1, keepdims=True), sink)
    p = jnp.exp(s - m).astype(BF16)
    acc = _dot(p, vx_ref[...])
    o = _finish_heads(acc, jnp.exp(sink - m)).astype(BF16)
    for t in range(SWA_GROUP):
        o_ref[:, LANES * t:LANES * (t + 1)] = o[ctx * t:ctx * (t + 1)]


def _swa(q, k, v, sink, nb, seq, ctx):
    rows = q.shape[0]
    n = seq + ctx
    nblk = seq // SWA_BLOCK
    bps = n // SWA_BLOCK
    gw = SWA_GROUP * LANES
    smem = pl.BlockSpec(memory_space=pltpu.SMEM)

    def kv_spec(off):
        return pl.BlockSpec((SWA_BLOCK, LANES),
                            lambda b, g, i: (b * bps + jnp.clip(i + off, 0, nblk - 1), g))
    ctx_spec = pl.BlockSpec((ctx, LANES), lambda b, g, i: (b * (n // ctx) + seq // ctx, g))
    qo_spec = pl.BlockSpec((SWA_BLOCK, gw), lambda b, g, i: (b * bps + i, g))
    out = pl.pallas_call(
        functools.partial(_swa_lat_kernel, nblk=nblk),
        grid=(nb, SWA_KV_HEADS, nblk),
        in_specs=[smem, qo_spec, kv_spec(-1), kv_spec(0), kv_spec(1), ctx_spec,
                  kv_spec(-1), kv_spec(0), kv_spec(1), ctx_spec],
        out_specs=qo_spec,
        out_shape=jax.ShapeDtypeStruct((rows, SWA_Q_HEADS * LANES), BF16),
        compiler_params=_params(("parallel", "parallel", "parallel")), name="swa_latent",
    )(sink, q, k, k, k, k, v, v, v, v)
    cq_spec = pl.BlockSpec((ctx, gw), lambda b, g: (b * (n // ctx) + seq // ctx, g))
    cx_spec = pl.BlockSpec((ctx, LANES), lambda b, g: (b * (n // ctx) + seq // ctx, g))
    return pl.pallas_call(
        functools.partial(_swa_ctx_kernel, ctx=ctx),
        grid=(nb, SWA_KV_HEADS),
        in_specs=[smem, cq_spec, cx_spec, cx_spec, pl.BlockSpec(memory_space=pl.ANY)],
        out_specs=cq_spec,
        out_shape=jax.ShapeDtypeStruct((rows, SWA_Q_HEADS * LANES), BF16),
        input_output_aliases={4: 0},
        compiler_params=_params(("parallel", "parallel")), name="swa_context",
    )(sink, q, k, v, out)


MLA_TQ = 512
MLA_TK = 512


def _mla_step(q, kc, vc, m, acc):
    s = _dot(q, kc, NT_DIMS)
    m_new = jnp.maximum(m, jnp.max(s, axis=-1, keepdims=True))
    alpha = jnp.exp(m - m_new)
    p = jnp.exp(s - m_new).astype(BF16)
    return m_new, alpha * acc + _dot(p, vc)


def _mla_lat_kernel(q_ref, k_ref, v_ref, o_ref, *, seq, ctx):
    q = q_ref[...]
    tq = q.shape[0]

    def body(c, carry):
        m, acc = carry
        start = pl.multiple_of(c * MLA_TK, MLA_TK)
        return _mla_step(q, k_ref[pl.ds(start, MLA_TK), :], v_ref[pl.ds(start, MLA_TK), :], m, acc)

    m0 = jnp.full((tq, 1), -jnp.inf, F32)
    acc0 = jnp.zeros((tq, LANES), F32)
    m, acc = lax.fori_loop(0, seq // MLA_TK, body, (m0, acc0))
    m, acc = _mla_step(q, k_ref[pl.ds(seq, ctx), :], v_ref[pl.ds(seq, ctx), :], m, acc)
    o_ref[...] = _finish_heads(acc, 0.0).astype(BF16)


def _mla_ctx_kernel(q_ref, k_ref, v_ref, prev_ref, o_ref):
    del prev_ref
    q = q_ref[...]
    m0 = jnp.full((q.shape[0], 1), -jnp.inf, F32)
    _, acc = _mla_step(q, k_ref[...], v_ref[...], m0, jnp.zeros((q.shape[0], LANES), F32))
    o_ref[...] = _finish_heads(acc, 0.0).astype(BF16)


def _mla(q, k, v, nb, seq, ctx):
    rows = q.shape[0]
    n = seq + ctx
    q3, k3, v3 = (t.reshape(nb, n, MLA_HEADS * LANES) for t in (q, k, v))
    tq = min(MLA_TQ, seq)
    kv_spec = pl.BlockSpec((None, n, LANES), lambda b, h, i: (b, 0, h))
    qo_spec = pl.BlockSpec((None, tq, LANES), lambda b, h, i: (b, i, h))
    out = pl.pallas_call(
        functools.partial(_mla_lat_kernel, seq=seq, ctx=ctx),
        grid=(nb, MLA_HEADS, seq // tq),
        in_specs=[qo_spec, kv_spec, kv_spec],
        out_specs=qo_spec,
        out_shape=jax.ShapeDtypeStruct((nb, n, MLA_HEADS * LANES), BF16),
        compiler_params=_params(("parallel", "parallel", "parallel")), name="mla_latent",
    )(q3, k3, v3)
    cx_spec = pl.BlockSpec((None, ctx, LANES), lambda b, h: (b, seq // ctx, h))
    out = pl.pallas_call(
        _mla_ctx_kernel,
        grid=(nb, MLA_HEADS),
        in_specs=[cx_spec, cx_spec, cx_spec, pl.BlockSpec(memory_space=pl.ANY)],
        out_specs=cx_spec,
        out_shape=jax.ShapeDtypeStruct((nb, n, MLA_HEADS * LANES), BF16),
        input_output_aliases={3: 0},
        compiler_params=_params(("parallel", "parallel")), name="mla_context",
    )(q3, k3, v3, out)
    return out.reshape(rows, MLA_HEADS * LANES)


def _hy_pre_kernel(u_ref, up_ref, un_ref, cw_ref, cb_ref, skip_ref, x0_ref, zs_ref, z_ref, *, ntb):
    i = pl.program_id(0)
    prev_ok, next_ok = _halo_valid(i, ntb)
    u = u_ref[...]
    r = lax.broadcasted_iota(jnp.int32, (TM, 1), 0)
    up_row = jnp.where(prev_ok, up_ref[7:8, :], 0.0)
    un_row = jnp.where(next_ok, un_ref[0:1, :], 0.0)
    um1 = jnp.where(r == 0, up_row, pltpu.roll(u, 1, 0))
    up1 = jnp.where(r == TM - 1, un_row, pltpu.roll(u, TM - 1, 0))
    uc = um1 * cw_ref[0:1, :] + u * cw_ref[1:2, :] + up1 * cw_ref[2:3, :] + cb_ref[...]
    w = HYENA_WIDTH
    x0 = uc[:, :w]
    z = uc[:, 2 * w:] * uc[:, w:2 * w]
    x0_ref[...] = x0
    zs_ref[...] = x0 * (skip_ref[...] * z)
    z_ref[...] = z.astype(BF16)


def _hy_pre(u, conv_w, conv_b, skip, ntb):
    rows = u.shape[0]
    w3 = 3 * HYENA_WIDTH
    prev_spec, next_spec = _halo_specs(w3, ntb)
    w = HYENA_WIDTH
    return pl.pallas_call(
        functools.partial(_hy_pre_kernel, ntb=ntb),
        grid=(rows // TM,),
        in_specs=[_tok_spec(w3), prev_spec, next_spec, _full_spec(conv_w.shape), _full_spec(conv_b.shape),
                  _full_spec(skip.shape)],
        out_specs=[_tok_spec(w), _tok_spec(w), _tok_spec(w)],
        out_shape=[jax.ShapeDtypeStruct((rows, w), F32), jax.ShapeDtypeStruct((rows, w), F32),
                   jax.ShapeDtypeStruct((rows, w), BF16)],
        compiler_params=_params(("parallel",)), name="hyena_pre",
    )(u, u, u, conv_w, conv_b, skip)


HIGHEST = lax.Precision.HIGHEST


def _filter_kernel(fr_ref, w1_ref, b1_ref, w2_ref, b2_ref, w3_ref, f0_ref, f1_ref, dl_ref, h_ref, n_ref,
                   *, length, tf):
    i = pl.program_id(0)
    pos = (i * tf + lax.broadcasted_iota(jnp.int32, (tf, 1), 0)).astype(F32)
    t = pos / max(length - 1, 1)
    lane = lax.broadcasted_iota(jnp.int32, (1, LANES), 1)
    bands = (HYENA_EMB - 1) // 2
    ang = (2 * math.pi / length) * pos * fr_ref[...]
    emb = jnp.where(lane == 0, t,
                    jnp.where(lane <= bands, jnp.cos(ang), jnp.where(lane <= 2 * bands, -jnp.sin(ang), 0.0)))
    h = jnp.sin(f0_ref[...] * (_dot(emb, w1_ref[...], precision=HIGHEST) + b1_ref[...]))
    h = jnp.sin(f1_ref[...] * (_dot(h, w2_ref[...], precision=HIGHEST) + b2_ref[...]))
    h = _dot(h, w3_ref[...], precision=HIGHEST)
    decay = jnp.exp(-t * dl_ref[...])
    hf = h[:, :HYENA_WIDTH] * decay
    hb = jnp.where(pos == 0.0, 0.0, h[:, HYENA_WIDTH:] * decay)
    h_ref[:, :HYENA_WIDTH] = hf
    h_ref[:, HYENA_WIDTH:] = hb

    @pl.when(i == 0)
    def _():
        n_ref[...] = jnp.zeros_like(n_ref)
    colsum = jnp.concatenate([jnp.sum(jnp.abs(hf), axis=0, keepdims=True),
                              jnp.sum(jnp.abs(hb), axis=0, keepdims=True)], axis=1)
    n_ref[...] += jnp.broadcast_to(colsum, n_ref.shape)


def _hyena_filter(length, fr, w1, b1, w2, b2, w3, f0, f1, deltas):
    tf = min(TM, length)
    consts = [fr, w1, b1, w2, b2, w3, f0, f1, deltas]
    return pl.pallas_call(
        functools.partial(_filter_kernel, length=length, tf=tf),
        grid=(length // tf,),
        in_specs=[_full_spec(c.shape) for c in consts],
        out_specs=[pl.BlockSpec((tf, 2 * HYENA_WIDTH), lambda i: (i, 0)),
                   pl.BlockSpec((8, 2 * HYENA_WIDTH), lambda i: (0, 0))],
        out_shape=[jax.ShapeDtypeStruct((length, 2 * HYENA_WIDTH), F32),
                   jax.ShapeDtypeStruct((8, 2 * HYENA_WIDTH), F32)],
        compiler_params=_params(("arbitrary",)), name="hyena_filter",
    )(*consts)


DFT_TN = 2048
DFT_KG = 8


def _dft1_kernel(f_ref, z_ref, ar_ref, ai_ref):
    a = _dot(f_ref[...], z_ref[...].astype(BF16))
    n1 = ar_ref.shape[0]
    ar_ref[...] = a[:n1].astype(BF16)
    ai_ref[...] = a[n1:].astype(BF16)


def _dft1(f1, z):
    g, _, cols = z.shape
    n1h = f1.shape[1]
    n1 = f1.shape[0] // 2
    tn = min(DFT_TN, cols)
    out_spec = pl.BlockSpec((None, n1, tn), lambda b, j: (b, 0, j))
    return pl.pallas_call(
        _dft1_kernel,
        grid=(g, cols // tn),
        in_specs=[pl.BlockSpec(f1.shape, lambda b, j: (0, 0)),
                  pl.BlockSpec((None, n1h, tn), lambda b, j: (b, 0, j))],
        out_specs=[out_spec, out_spec],
        out_shape=[jax.ShapeDtypeStruct((g, n1, cols), BF16)] * 2,
        compiler_params=_params(("parallel", "parallel")), name="hyena_dft_outer",
    )(f1, z)


def _spec_filter_kernel(m_ref, ar_ref, ai_ref, n_ref, gr_ref, gi_ref, *, inv_n):
    w = HYENA_WIDTH
    nrm = n_ref[0:1, :w] + n_ref[0:1, w:]
    inv = inv_n / nrm
    for kk in range(m_ref.shape[0]):
        x = jnp.concatenate([ar_ref[kk], ai_ref[kk]], axis=0)
        y = _dot(m_ref[kk], x)
        n2 = y.shape[0] // 2
        gr_ref[kk] = (y[:n2, :w] + y[:n2, w:]) * inv
        gi_ref[kk] = (y[n2:, :w] - y[n2:, w:]) * inv


def _spec_kernel(m_ref, ar_ref, ai_ref, gr_ref, gi_ref, cr_ref, ci_ref):
    for kk in range(m_ref.shape[0]):
        mk = m_ref[kk]
        gr, gi = gr_ref[kk], gi_ref[kk]
        for b in range(ar_ref.shape[0]):
            x = jnp.concatenate([ar_ref[b, kk], ai_ref[b, kk]], axis=0)
            y = _dot(mk, x)
            n2 = y.shape[0] // 2
            yr, yi = y[:n2], y[n2:]
            p = jnp.concatenate([yr * gr - yi * gi, yr * gi + yi * gr], axis=0).astype(BF16)
            c = _dot(mk, p, TN_DIMS)
            cr_ref[b, kk] = c[:n2].astype(BF16)
            ci_ref[b, kk] = c[n2:].astype(BF16)


def _spec_filter(mtab, ar, ai, nrm, n_total):
    n1 = mtab.shape[0]
    kg = min(DFT_KG, n1)
    w2 = ar.shape[-1]
    a_spec = pl.BlockSpec((kg, DFT_N2, w2), lambda j: (j, 0, 0))
    g_spec = pl.BlockSpec((kg, DFT_N2, HYENA_WIDTH), lambda j: (j, 0, 0))
    return pl.pallas_call(
        functools.partial(_spec_filter_kernel, inv_n=1.0 / n_total),
        grid=(n1 // kg,),
        in_specs=[pl.BlockSpec((kg, 2 * DFT_N2, 2 * DFT_N2), lambda j: (j, 0, 0)), a_spec, a_spec,
                  _full_spec(nrm.shape)],
        out_specs=[g_spec, g_spec],
        out_shape=[jax.ShapeDtypeStruct((n1, DFT_N2, HYENA_WIDTH), F32)] * 2,
        compiler_params=_params(("parallel",)), name="hyena_filter_spectrum",
    )(mtab, ar, ai, nrm)


def _spec(mtab, ar, ai, gr, gi):
    nb, n1 = ar.shape[0], ar.shape[1]
    kg = min(DFT_KG, n1)
    w = HYENA_WIDTH
    a_spec = pl.BlockSpec((nb, kg, DFT_N2, w), lambda j: (0, j, 0, 0))
    g_spec = pl.BlockSpec((kg, DFT_N2, w), lambda j: (j, 0, 0))
    return pl.pallas_call(
        _spec_kernel,
        grid=(n1 // kg,),
        in_specs=[pl.BlockSpec((kg, 2 * DFT_N2, 2 * DFT_N2), lambda j: (j, 0, 0)), a_spec, a_spec, g_spec, g_spec],
        out_specs=[a_spec, a_spec],
        out_shape=[jax.ShapeDtypeStruct(ar.shape, BF16)] * 2,
        compiler_params=_params(("parallel",)), name="hyena_spectrum",
    )(mtab, ar, ai, gr, gi)


def _idft_kernel(f_ref, cr_ref, ci_ref, x0_ref, zs_ref, o_ref):
    c = jnp.concatenate([cr_ref[...], ci_ref[...]], axis=0)
    y = _dot(f_ref[...], c)
    o_ref[...] = (x0_ref[...] * y + zs_ref[...]).astype(BF16)


def _idft(finv, cr, ci, x0v, zsv):
    nb, n1, cols = cr.shape
    n1h = n1 // 2
    tn = min(DFT_TN, cols)
    c_spec = pl.BlockSpec((None, n1, tn), lambda b, j: (b, 0, j))
    t_spec = pl.BlockSpec((None, n1h, tn), lambda b, j: (b, 0, j))
    return pl.pallas_call(
        _idft_kernel,
        grid=(nb, cols // tn),
        in_specs=[pl.BlockSpec(finv.shape, lambda b, j: (0, 0)), c_spec, c_spec, t_spec, t_spec],
        out_specs=t_spec,
        out_shape=jax.ShapeDtypeStruct(x0v.shape, BF16),
        compiler_params=_params(("parallel", "parallel")), name="hyena_idft_outer",
    )(finv, cr, ci, x0v, zsv)


def _hy_ctx_kernel(fc_ref, fi_ref, z_ref, h_ref, x0_ref, zs_ref, prev_ref, o_ref, *, n_total):
    del prev_ref
    w = HYENA_WIDTH
    h = h_ref[...]
    nk = fc_ref.shape[0] // 2
    nrm = jnp.sum(jnp.abs(h[:, :w]), axis=0, keepdims=True) + jnp.sum(jnp.abs(h[:, w:]), axis=0, keepdims=True)
    inv = (1.0 / n_total) / nrm
    hs = _dot(fc_ref[...], h.astype(BF16))
    gr = (hs[:nk, :w] + hs[:nk, w:]) * inv
    gi = (hs[nk:, :w] - hs[nk:, w:]) * inv
    zsp = _dot(fc_ref[...], z_ref[...])
    zr, zi = zsp[:nk], zsp[nk:]
    p = jnp.concatenate([zr * gr - zi * gi, zr * gi + zi * gr], axis=0).astype(BF16)
    y = _dot(fi_ref[...], p)
    o_ref[...] = (x0_ref[...] * y + zs_ref[...]).astype(BF16)


def _hy_ctx(fc, fi, z, hcat, x0, zs, out, nb, seq, ctx):
    n = seq + ctx
    w = HYENA_WIDTH
    row_spec = pl.BlockSpec((ctx, w), lambda b: (b * (n // ctx) + seq // ctx, 0))
    return pl.pallas_call(
        functools.partial(_hy_ctx_kernel, n_total=2 * ctx),
        grid=(nb,),
        in_specs=[_full_spec(fc.shape), _full_spec(fi.shape), row_spec, _full_spec(hcat.shape), row_spec, row_spec,
                  pl.BlockSpec(memory_space=pl.ANY)],
        out_specs=row_spec,
        out_shape=jax.ShapeDtypeStruct(out.shape, BF16),
        input_output_aliases={6: 0},
        compiler_params=_params(("parallel",)), name="hyena_context",
    )(fc, fi, z, hcat, x0, zs, out)


def _dft_tables(seq, ctx):
    n = 2 * seq
    n1 = n // DFT_N2
    two_pi = 2.0 * math.pi

    def cs(num, den):
        ang = two_pi * (num % den).astype(F32) / den
        return jnp.cos(ang), jnp.sin(ang)
    k1 = jnp.arange(n1, dtype=jnp.int32)
    c, s = cs(k1[:, None] * k1[None, :n1 // 2], n1)
    f1 = jnp.concatenate([c, -s], axis=0).astype(BF16)
    finv = jnp.concatenate([c.T, -s.T], axis=1)
    k2 = jnp.arange(DFT_N2, dtype=jnp.int32)
    phase = k2[None, None, :] * (k1[:, None, None] + n1 * k2[None, :, None])
    c, s = cs(phase, n)
    mtab = jnp.concatenate([jnp.concatenate([c, s], axis=2), jnp.concatenate([-s, c], axis=2)], axis=1)
    nc = 2 * ctx
    kk = jnp.arange(nc, dtype=jnp.int32)
    c, s = cs(kk[:, None] * kk[None, :ctx], nc)
    fc = jnp.concatenate([c, -s], axis=0).astype(BF16)
    fi = jnp.concatenate([c.T, -s.T], axis=1).astype(BF16)
    return f1, finv.astype(BF16), mtab.astype(BF16), fc, fi


def _post_kernel(a1_ref, a2_ref, w1_ref, w2_ref, x_ref, mod_ref, g_ref, rw_ref, rb_ref,
                 xo_ref, h_ref, comb_ref):
    ml = _dot(a1_ref[...], w1_ref[...]) + _dot(a2_ref[...], w2_ref[...])
    x = x_ref[...] + mod_ref[2:3, :] * ml
    xo_ref[...] = x
    h = _modulate(x, g_ref[...], mod_ref[3:4, :], mod_ref[4:5, :]).astype(BF16)
    h_ref[...] = h
    scores = jax.nn.sigmoid(_dot(rw_ref[...], h, NT_DIMS))
    biased = scores + rb_ref[...]
    sc = [scores[e:e + 1, :] for e in range(N_EXPERTS)]
    bi = [biased[e:e + 1, :] for e in range(N_EXPERTS)]
    best = None
    sel = None
    for g in range(N_GROUPS):
        v0, v1, v2, v3 = bi[4 * g:4 * g + 4]
        top1 = jnp.maximum(jnp.maximum(v0, v1), jnp.maximum(v2, v3))
        top2 = jnp.maximum(jnp.maximum(jnp.minimum(v0, v1), jnp.minimum(v2, v3)),
                           jnp.minimum(jnp.maximum(v0, v1), jnp.maximum(v2, v3)))
        gs = top1 + top2
        if g == 0:
            best, sel = gs, jnp.zeros_like(gs, dtype=jnp.int32)
        else:
            upd = gs > best
            best = jnp.where(upd, gs, best)
            sel = jnp.where(upd, g, sel)

    def pick(vals, j):
        out = vals[j]
        for g in range(1, N_GROUPS):
            out = jnp.where(sel == g, vals[4 * g + j], out)
        return out
    b = [pick(bi, j) for j in range(EXPERTS_PER_GROUP)]
    s = [pick(sc, j) for j in range(EXPERTS_PER_GROUP)]
    i1 = jnp.zeros_like(sel)
    m1 = b[0]
    for j in range(1, EXPERTS_PER_GROUP):
        upd = b[j] > m1
        m1 = jnp.where(upd, b[j], m1)
        i1 = jnp.where(upd, j, i1)
    i2 = jnp.full_like(sel, -1)
    m2 = jnp.full_like(m1, -jnp.inf)
    for j in range(EXPERTS_PER_GROUP):
        upd = jnp.logical_and(i1 != j, b[j] > m2)
        m2 = jnp.where(upd, b[j], m2)
        i2 = jnp.where(upd, j, i2)
    w1 = s[0]
    w2 = s[0]
    for j in range(1, EXPERTS_PER_GROUP):
        w1 = jnp.where(i1 == j, s[j], w1)
        w2 = jnp.where(i2 == j, s[j], w2)
    tot = w1 + w2
    e1 = sel * EXPERTS_PER_GROUP + i1
    e2 = sel * EXPERTS_PER_GROUP + i2
    erow = lax.broadcasted_iota(jnp.int32, (LANES, 1), 0)
    comb_t = jnp.where(erow == e1, w1 / tot, 0.0) + jnp.where(erow == e2, w2 / tot, 0.0)
    comb_ref[...] = comb_t.T


def _post(a1, a2, w1, w2, x, mod, g, rw, rb, ntb):
    rows, d = x.shape
    row = _mod_row(ntb)
    mod_spec = pl.BlockSpec((None, 8, d), lambda i: (row(i), 0, 0))
    consts = [w1, w2]
    return pl.pallas_call(
        _post_kernel,
        grid=(rows // TM,),
        in_specs=[_tok_spec(a1.shape[1]), _tok_spec(a2.shape[1]), _full_spec(w1.shape), _full_spec(w2.shape),
                  _tok_spec(d), mod_spec, _full_spec(g.shape), _full_spec(rw.shape), _full_spec(rb.shape)],
        out_specs=[_tok_spec(d), _tok_spec(d), _tok_spec(LANES)],
        out_shape=[jax.ShapeDtypeStruct((rows, d), F32), jax.ShapeDtypeStruct((rows, d), BF16),
                   jax.ShapeDtypeStruct((rows, LANES), F32)],
        compiler_params=_params(("parallel",)), name="post_mixer",
    )(a1, a2, *consts, x, mod, g, rw, rb)


MOE_TM = 1280


def _moe_kernel(h_ref, comb_ref, wgu_ref, wd_ref, y_ref, acc_ref):
    e = pl.program_id(1)

    @pl.when(e == 0)
    def _():
        acc_ref[...] = jnp.zeros_like(acc_ref)
    h = h_ref[...]
    gu = _dot(h, wgu_ref[...])
    gate, up = gu[:, :EXPERT_FF], gu[:, EXPERT_FF:]
    a = (gate * jax.nn.sigmoid(gate) * up).astype(BF16)
    out = _dot(a, wd_ref[...])
    lane = lax.broadcasted_iota(jnp.int32, (1, LANES), 1)
    ce = jnp.sum(jnp.where(lane == e, comb_ref[...], 0.0), axis=-1, keepdims=True)
    acc_ref[...] += ce * out

    @pl.when(e == N_EXPERTS - 1)
    def _():
        y_ref[...] = acc_ref[...]


def _moe(h, comb, wgu, wd):
    rows, d = h.shape
    tm = MOE_TM if rows % MOE_TM == 0 else TM
    return pl.pallas_call(
        _moe_kernel,
        grid=(rows // tm, N_EXPERTS),
        in_specs=[pl.BlockSpec((tm, d), lambda i, e: (i, 0)), pl.BlockSpec((tm, LANES), lambda i, e: (i, 0)),
                  pl.BlockSpec((None, d, 2 * EXPERT_FF), lambda i, e: (e, 0, 0)),
                  pl.BlockSpec((None, EXPERT_FF, d), lambda i, e: (e, 0, 0))],
        out_specs=pl.BlockSpec((tm, d), lambda i, e: (i, 0)),
        out_shape=jax.ShapeDtypeStruct((rows, d), F32),
        scratch_shapes=[pltpu.VMEM((tm, d), F32)],
        compiler_params=_params(("parallel", "arbitrary")), name="moe_experts",
    )(h, comb, wgu, wd)


def _final_kernel(x_ref, y_ref, mod_ref, o_ref):
    o_ref[...] = x_ref[...] + mod_ref[5:6, :] * y_ref[...]


def _final(x, y, mod, nb, seq, ctx):
    d = x.shape[1]
    ntb = (seq + ctx) // TM
    nlt = seq // TM
    tok = pl.BlockSpec((TM, d), lambda b, j: (b * ntb + j, 0))
    return pl.pallas_call(
        _final_kernel,
        grid=(nb, nlt),
        in_specs=[tok, tok, pl.BlockSpec((None, 8, d), lambda b, j: (b, 0, 0))],
        out_specs=pl.BlockSpec((None, TM, d), lambda b, j: (b, j, 0)),
        out_shape=jax.ShapeDtypeStruct((nb, seq, d), F32),
        compiler_params=_params(("parallel", "parallel")), name="final_residual",
    )(x, y, mod)


def _pad_heads(w, heads, dim, axis):
    shp = w.shape
    w = w.reshape(shp[:axis] + (heads, dim) + shp[axis + 1:])
    pad = [(0, 0)] * w.ndim
    pad[axis + 1] = (0, LANES - dim)
    w = jnp.pad(w, pad)
    return w.reshape(shp[:axis] + (heads * LANES,) + shp[axis + 1:])


def _pad_vec(v, mult=1.0):
    return jnp.pad(v.astype(F32) * mult, (0, LANES - v.shape[0])).reshape(1, LANES)


def _rope_tables(seq, ctx, d_rot, off):
    rows = seq // GRID_W
    row = jnp.repeat(jnp.arange(rows), GRID_W).astype(F32)
    col = jnp.tile(jnp.arange(GRID_W), rows).astype(F32)
    n_freq = d_rot // 4
    inv = ROPE_BASE ** (-jnp.arange(n_freq, dtype=F32) / n_freq)
    ang = jnp.concatenate([row[:, None] * inv, col[:, None] * inv], axis=-1)
    cos, sin = jnp.cos(ang), jnp.sin(ang)
    half = d_rot // 2

    def z(r, w):
        return jnp.zeros((r, w), F32)
    rest = LANES - off - 2 * half
    cos_l = jnp.concatenate([jnp.ones((seq, off), F32), cos, cos, z(seq, rest)], axis=1)
    sa_l = jnp.concatenate([z(seq, off + half), sin, z(seq, rest)], axis=1)
    sb_l = jnp.concatenate([z(seq, off), -sin, z(seq, half + rest)], axis=1)
    cos_c = jnp.concatenate([jnp.ones((ctx, off + 2 * half), F32), z(ctx, rest)], axis=1)
    return (jnp.concatenate([cos_l, cos_c], axis=0), jnp.concatenate([sa_l, z(ctx, LANES)], axis=0),
            jnp.concatenate([sb_l, z(ctx, LANES)], axis=0))


def _forward(x, c, ctx, c_ctx, ada_w, ada_b, norm_g, ev_w_in, ev_w_out, pool_w, pool_scale,
             swa_q_gain, swa_k_gain, swa_sink, od_w_in, od_w_out, hy_conv_w, hy_conv_b,
             hy_w1, hy_b1, hy_w2, hy_b2, hy_w3, hy_freq, hy_skip, mla_cq_gain, mla_ckv_gain,
             mla_w_uq, mla_w_ukv, mla_q_gain, mla_k_gain, router_w, router_b,
             moe_w_gate, moe_w_up, moe_w_down):
    nb, seq, d = x.shape
    nctx = ctx.shape[1]
    depth = ada_w.shape[0]
    assert nctx == TM and seq % (2 * TM) == 0 and seq % GRID_W == 0 and nb <= 2
    n = seq + nctx
    ntb = n // TM
    rows = nb * n

    cvec = jnp.concatenate([c, c_ctx[None, :], jnp.zeros((8 - nb - 1, d), F32)], axis=0)
    if nb == 1:
        cvec = jnp.concatenate([c, jnp.zeros((1, d), F32), c_ctx[None, :], jnp.zeros((5, d), F32)], axis=0)
    mod = _adaln(cvec, ada_w, ada_b)

    xs = jnp.concatenate([x, ctx], axis=1).reshape(rows, d)
    tabs_swa = _rope_tables(seq, nctx, HEAD_DIM, 0)
    tabs_mla = _rope_tables(seq, nctx, MLA_ROPE, MLA_NOPE)
    f1, finv, mtab, fc, fi = _dft_tables(seq, nctx)
    bands = (HYENA_EMB - 1) // 2
    frv = jnp.linspace(1e-4, bands - 1, bands, dtype=F32)
    fr = jnp.concatenate([jnp.zeros((1,), F32), frv, frv, jnp.zeros((LANES - 1 - 2 * bands,), F32)]).reshape(1, LANES)
    deltas = jnp.abs(jnp.linspace(math.log(HYENA_TARGET) / HYENA_FAST_DECAY,
                                  math.log(HYENA_TARGET) / HYENA_SLOW_DECAY, HYENA_WIDTH, dtype=F32)).reshape(1, -1)
    rw = jnp.transpose(router_w).astype(BF16)
    rb = router_b.astype(F32).reshape(N_EXPERTS, 1)

    prev = None
    y = None
    for layer in range(depth):
        i = layer // 2
        lmod = mod[layer]
        g1 = norm_g[layer, 0].reshape(1, d)
        g2 = norm_g[layer, 1].reshape(1, d)
        if layer % 2 == 0:
            w = ev_w_in[i]
            o1 = POOL_WIDTH
            o2 = o1 + SWA_Q_HEADS * HEAD_DIM
            o3 = o2 + SWA_KV_HEADS * HEAD_DIM
            w_in = jnp.concatenate([w[:, :o1], _pad_heads(w[:, o1:o2], SWA_Q_HEADS, HEAD_DIM, 1),
                                    _pad_heads(w[:, o2:o3], SWA_KV_HEADS, HEAD_DIM, 1),
                                    _pad_heads(w[:, o3:], SWA_KV_HEADS, HEAD_DIM, 1)], axis=1).astype(BF16)
            qg = _pad_vec(swa_q_gain[i], HEAD_DIM ** -0.5)
            kg = _pad_vec(swa_k_gain[i])
            xs, a, q, k, v = _even_in(xs, prev, lmod, g1, w_in, qg, kg, tabs_swa, ntb)
            w_bd = jnp.zeros((POOL_WIDTH, POOL_WIDTH), F32)
            for g in range(POOL_GROUPS):
                sl = slice(g * POOL_GROUP_DIM, (g + 1) * POOL_GROUP_DIM)
                w_bd = w_bd.at[sl, sl].set(pool_w[i, g])
            mix1 = _pool(a, w_bd.astype(BF16), pool_scale[i].reshape(1, -1), ntb, seq, nctx)
            mix2 = _swa(q, k, v, swa_sink[i].astype(F32), nb, seq, nctx)
            wo = ev_w_out[i]
            wo1 = wo[:POOL_WIDTH].astype(BF16)
            wo2 = _pad_heads(wo[POOL_WIDTH:], SWA_Q_HEADS, HEAD_DIM, 0).astype(BF16)
        else:
            w_in = jnp.pad(od_w_in[i], ((0, 0), (0, LANES - MLA_ROPE))).astype(BF16)
            wuq = _pad_heads(mla_w_uq[i], MLA_HEADS, MLA_QK, 1).astype(BF16)
            wukv = mla_w_ukv[i].reshape(MLA_KV_RANK, MLA_HEADS, MLA_NOPE + MLA_V)
            wuk = _pad_heads(wukv[:, :, :MLA_NOPE].reshape(MLA_KV_RANK, -1), MLA_HEADS, MLA_NOPE, 1).astype(BF16)
            wuv = _pad_heads(wukv[:, :, MLA_NOPE:].reshape(MLA_KV_RANK, -1), MLA_HEADS, MLA_V, 1).astype(BF16)
            qg = _pad_vec(mla_q_gain[i], MLA_QK ** -0.5)
            kg = _pad_vec(mla_k_gain[i])
            xs, u, q, k, v = _odd_in(xs, prev, lmod, g1, w_in, mla_cq_gain[i].reshape(1, -1),
                                     mla_ckv_gain[i].reshape(1, -1), wuq, wuk, wuv, qg, kg, tabs_mla, ntb)
            x0, zs, z = _hy_pre(u, hy_conv_w[i], hy_conv_b[i].reshape(1, -1), hy_skip[i].reshape(1, -1), ntb)
            w1p = jnp.zeros((LANES, LANES), F32).at[:HYENA_EMB, :HYENA_HIDDEN].set(hy_w1[i])
            w2p = jnp.zeros((LANES, LANES), F32).at[:HYENA_HIDDEN, :HYENA_HIDDEN].set(hy_w2[i])
            w3p = jnp.zeros((LANES, 2 * HYENA_WIDTH), F32).at[:HYENA_HIDDEN].set(hy_w3[i])
            fparams = (fr, w1p, _pad_vec(hy_b1[i]), w2p, _pad_vec(hy_b2[i]), w3p,
                       _pad_vec(hy_freq[i, 0]), _pad_vec(hy_freq[i, 1]), deltas)
            hcat, nrm = _hyena_filter(seq, *fparams)
            hcat_c, _ = _hyena_filter(nctx, *fparams)
            n1 = 2 * seq // DFT_N2
            w = HYENA_WIDTH
            har, hai = _dft1(f1, hcat.astype(BF16).reshape(1, n1 // 2, DFT_N2 * 2 * w))
            gr, gi = _spec_filter(mtab, har.reshape(n1, DFT_N2, 2 * w), hai.reshape(n1, DFT_N2, 2 * w),
                                  nrm, 2 * seq)
            ar, ai = _dft1(f1, z.reshape(nb, n // DFT_N2, DFT_N2 * w))
            cr, ci = _spec(mtab, ar.reshape(nb, n1, DFT_N2, w), ai.reshape(nb, n1, DFT_N2, w), gr, gi)
            view = (nb, n // DFT_N2, DFT_N2 * w)
            hy = _idft(finv, cr.reshape(nb, n1, DFT_N2 * w), ci.reshape(nb, n1, DFT_N2 * w),
                       x0.reshape(view), zs.reshape(view))
            mix1 = _hy_ctx(fc, fi, z, hcat_c, x0, zs, hy.reshape(rows, w), nb, seq, nctx)
            mix2 = _mla(q, k, v, nb, seq, nctx)
            wo = od_w_out[i]
            wo1 = wo[:HYENA_WIDTH].astype(BF16)
            wo2 = _pad_heads(wo[HYENA_WIDTH:], MLA_HEADS, MLA_V, 0).astype(BF16)
        xs, h2, comb = _post(mix1, mix2, wo1, wo2, xs, lmod, g2, rw, rb, ntb)
        wgu = jnp.concatenate([moe_w_gate[layer], moe_w_up[layer]], axis=-1).astype(BF16)
        y = _moe(h2, comb, wgu, moe_w_down[layer].astype(BF16))
        prev = (y, lmod)
    return _final(xs, y, mod[depth - 1], nb, seq, nctx)


def kernel(x, c, ctx, c_ctx, ada_w, ada_b, norm_g, ev_w_in, ev_w_out, pool_w, pool_scale, swa_q_gain, swa_k_gain, swa_sink, od_w_in, od_w_out, hy_conv_w, hy_conv_b, hy_w1, hy_b1, hy_w2, hy_b2, hy_w3, hy_freq, hy_skip, mla_cq_gain, mla_ckv_gain, mla_w_uq, mla_w_ukv, mla_q_gain, mla_k_gain, router_w, router_b, moe_w_gate, moe_w_up, moe_w_down):
    return _forward(x, c, ctx, c_ctx, ada_w, ada_b, norm_g, ev_w_in, ev_w_out, pool_w, pool_scale,
                    swa_q_gain, swa_k_gain, swa_sink, od_w_in, od_w_out, hy_conv_w, hy_conv_b,
                    hy_w1, hy_b1, hy_w2, hy_b2, hy_w3, hy_freq, hy_skip, mla_cq_gain, mla_ckv_gain,
                    mla_w_uq, mla_w_ukv, mla_q_gain, mla_k_gain, router_w, router_b,
                    moe_w_gate, moe_w_up, moe_w_down)
```

```python
import functools
import math

import jax
import jax.numpy as jnp
from jax import lax
from jax.experimental import pallas as pl
from jax.experimental.pallas import tpu as pltpu

F32 = jnp.float32
BF16 = jnp.bfloat16

GRID_W = 64
HEAD_DIM = 64
ROPE_BASE = 10000.0
EPS = 1e-6
POOL_GROUPS = 4
POOL_GROUP_DIM = 64
POOL_WIDTH = POOL_GROUPS * POOL_GROUP_DIM
POOL_WINDOWS = (2, 4, 8, 16)
SWA_Q_HEADS = 12
SWA_KV_HEADS = 4
SWA_GROUP = SWA_Q_HEADS // SWA_KV_HEADS
SWA_WINDOW = 128
SWA_BLOCK = 128
HYENA_WIDTH = 512
HYENA_EMB = 33
HYENA_HIDDEN = 64
HYENA_FAST_DECAY = 0.3
HYENA_SLOW_DECAY = 1.5
HYENA_TARGET = 1e-2
MLA_HEADS = 8
MLA_NOPE = 64
MLA_ROPE = 32
MLA_QK = MLA_NOPE + MLA_ROPE
MLA_V = 64
MLA_Q_RANK = 256
MLA_KV_RANK = 128
N_EXPERTS = 16
N_GROUPS = 4
EXPERTS_PER_GROUP = N_EXPERTS // N_GROUPS
EXPERT_FF = 512

LANES = 128
TM = 256
DFT_N2 = 128
VMEM_LIMIT = 56 * 1024 * 1024

NT_DIMS = (((1,), (1,)), ((), ()))
TN_DIMS = (((0,), (0,)), ((), ()))


def _dot(a, b, dims=None, precision=None):
    if dims is None:
        return jnp.dot(a, b, preferred_element_type=F32, precision=precision)
    return lax.dot_general(a, b, dims, preferred_element_type=F32, precision=precision)


def _params(sem):
    return pltpu.CompilerParams(dimension_semantics=sem, vmem_limit_bytes=VMEM_LIMIT)


def _mod_row(ntb):
    def f(i):
        return jnp.where(i % ntb == ntb - 1, 2, i // ntb)
    return f


def _modulate(x, g, shift, scale):
    ms = jnp.mean(x * x, axis=-1, keepdims=True)
    return (x * lax.rsqrt(ms + EPS) * g) * (1.0 + scale) + shift


def _head_norm_rope(xh, real_dim, gain, cos, sa, sb, half):
    r = lax.rsqrt(jnp.sum(xh * xh, axis=-1, keepdims=True) * (1.0 / real_dim) + EPS)
    xn = xh * r * gain
    return xn * cos + pltpu.roll(xn, half, 1) * sa + pltpu.roll(xn, LANES - half, 1) * sb


def _adaln_kernel(c_ref, w_ref, b_ref, o_ref):
    c = c_ref[...]
    s = (c * jax.nn.sigmoid(c)).astype(BF16)
    o_ref[...] = _dot(s, w_ref[...].astype(BF16)) + b_ref[...]


def _adaln(cvec, ada_w, ada_b):
    depth, d, d6 = ada_w.shape
    nchunk = d6 // d
    out = pl.pallas_call(
        _adaln_kernel,
        grid=(depth, nchunk),
        in_specs=[pl.BlockSpec((8, d), lambda l, j: (0, 0)),
                  pl.BlockSpec((None, d, d), lambda l, j: (l, 0, j)),
                  pl.BlockSpec((None, 1, d), lambda l, j: (l, 0, j))],
        out_specs=pl.BlockSpec((None, None, 8, d), lambda l, j: (l, j, 0, 0)),
        out_shape=jax.ShapeDtypeStruct((depth, nchunk, 8, d), F32),
        compiler_params=_params(("parallel", "parallel")),
        name="adaln",
    )(cvec, ada_w, ada_b.reshape(depth, 1, d6))
    mod = jnp.transpose(out, (0, 2, 1, 3))
    return jnp.pad(mod, ((0, 0), (0, 0), (0, 8 - nchunk), (0, 0)))


def _even_in_kernel(*refs, has_prev):
    if has_prev:
        (x_ref, y_ref, pmod_ref, mod_ref, g_ref, w_ref, qg_ref, kg_ref, cos_ref, sa_ref, sb_ref,
         xo_ref, a_ref, q_ref, k_ref, v_ref) = refs
        x = x_ref[...] + pmod_ref[5:6, :] * y_ref[...]
        xo_ref[...] = x
    else:
        (x_ref, mod_ref, g_ref, w_ref, qg_ref, kg_ref, cos_ref, sa_ref, sb_ref,
         a_ref, q_ref, k_ref, v_ref) = refs
        x = x_ref[...]
    h = _modulate(x, g_ref[...], mod_ref[0:1, :], mod_ref[1:2, :]).astype(BF16)
    p = _dot(h, w_ref[...])
    a_ref[...] = p[:, :POOL_WIDTH]
    cos, sa, sb = cos_ref[...], sa_ref[...], sb_ref[...]
    o = POOL_WIDTH
    for hh in range(SWA_Q_HEADS):
        xh = p[:, o + LANES * hh:o + LANES * (hh + 1)]
        q_ref[:, LANES * hh:LANES * (hh + 1)] = _head_norm_rope(
            xh, HEAD_DIM, qg_ref[...], cos, sa, sb, HEAD_DIM // 2).astype(BF16)
    o += SWA_Q_HEADS * LANES
    for hh in range(SWA_KV_HEADS):
        xh = p[:, o + LANES * hh:o + LANES * (hh + 1)]
        k_ref[:, LANES * hh:LANES * (hh + 1)] = _head_norm_rope(
            xh, HEAD_DIM, kg_ref[...], cos, sa, sb, HEAD_DIM // 2).astype(BF16)
    o += SWA_KV_HEADS * LANES
    ones_hi = (lax.broadcasted_iota(jnp.int32, (1, LANES), 1) >= HEAD_DIM).astype(F32)
    for hh in range(SWA_KV_HEADS):
        vh = p[:, o + LANES * hh:o + LANES * (hh + 1)]
        v_ref[:, LANES * hh:LANES * (hh + 1)] = (vh + ones_hi).astype(BF16)


def _odd_in_kernel(x_ref, y_ref, pmod_ref, mod_ref, g_ref, w_ref, cqg_ref, ckvg_ref, wuq_ref, wuk_ref,
                   wuv_ref, qg_ref, kg_ref, cos_ref, sa_ref, sb_ref,
                   xo_ref, u_ref, q_ref, k_ref, v_ref):
    x = x_ref[...] + pmod_ref[5:6, :] * y_ref[...]
    xo_ref[...] = x
    h = _modulate(x, g_ref[...], mod_ref[0:1, :], mod_ref[1:2, :]).astype(BF16)
    p = _dot(h, w_ref[...])
    nu = 3 * HYENA_WIDTH
    u_ref[...] = p[:, :nu]
    cq = p[:, nu:nu + MLA_Q_RANK]
    ckv = p[:, nu + MLA_Q_RANK:nu + MLA_Q_RANK + MLA_KV_RANK]
    krb = p[:, nu + MLA_Q_RANK + MLA_KV_RANK:]
    cqn = (cq * lax.rsqrt(jnp.mean(cq * cq, axis=-1, keepdims=True) + EPS) * cqg_ref[...]).astype(BF16)
    ckvn = (ckv * lax.rsqrt(jnp.mean(ckv * ckv, axis=-1, keepdims=True) + EPS) * ckvg_ref[...]).astype(BF16)
    qp = _dot(cqn, wuq_ref[...])
    kp = _dot(ckvn, wuk_ref[...])
    vp = _dot(ckvn, wuv_ref[...])
    krp = pltpu.roll(krb, MLA_NOPE, 1)
    cos, sa, sb = cos_ref[...], sa_ref[...], sb_ref[...]
    ones_hi = (lax.broadcasted_iota(jnp.int32, (1, LANES), 1) >= MLA_V).astype(F32)
    for hh in range(MLA_HEADS):
        sl = slice(LANES * hh, LANES * (hh + 1))
        q_ref[:, sl] = _head_norm_rope(qp[:, sl], MLA_QK, qg_ref[...], cos, sa, sb,
                                       MLA_ROPE // 2).astype(BF16)
        k_ref[:, sl] = _head_norm_rope(kp[:, sl] + krp, MLA_QK, kg_ref[...], cos, sa, sb,
                                       MLA_ROPE // 2).astype(BF16)
        v_ref[:, sl] = (vp[:, sl] + ones_hi).astype(BF16)


def _tok_spec(width):
    return pl.BlockSpec((TM, width), lambda i: (i, 0))


def _full_spec(shape):
    nd = len(shape)
    return pl.BlockSpec(shape, lambda i: (0,) * nd)


def _even_in(x, prev, mod, g, w, qg, kg, tabs, ntb):
    rows, d = x.shape
    row = _mod_row(ntb)
    mod_spec = pl.BlockSpec((None, 8, d), lambda i: (row(i), 0, 0))
    tab_spec = pl.BlockSpec((TM, LANES), lambda i: (i % ntb, 0))
    nq, nk = SWA_Q_HEADS * LANES, SWA_KV_HEADS * LANES
    ins = [x]
    specs = [_tok_spec(d)]
    outs = []
    ospecs = []
    if prev is not None:
        y, pmod = prev
        ins += [y, pmod]
        specs += [_tok_spec(d), mod_spec]
        outs.append(jax.ShapeDtypeStruct((rows, d), F32))
        ospecs.append(_tok_spec(d))
    ins += [mod, g, w, qg, kg, *tabs]
    specs += [mod_spec, _full_spec(g.shape), _full_spec(w.shape), _full_spec(qg.shape), _full_spec(kg.shape),
              tab_spec, tab_spec, tab_spec]
    outs += [jax.ShapeDtypeStruct((rows, POOL_WIDTH), F32), jax.ShapeDtypeStruct((rows, nq), BF16),
             jax.ShapeDtypeStruct((rows, nk), BF16), jax.ShapeDtypeStruct((rows, nk), BF16)]
    ospecs += [_tok_spec(POOL_WIDTH), _tok_spec(nq), _tok_spec(nk), _tok_spec(nk)]
    res = pl.pallas_call(
        functools.partial(_even_in_kernel, has_prev=prev is not None),
        grid=(rows // TM,), in_specs=specs, out_specs=ospecs, out_shape=outs,
        compiler_params=_params(("parallel",)), name="even_in",
    )(*ins)
    if prev is None:
        return (x, *res)
    return res


def _odd_in(x, prev, mod, g, w, cqg, ckvg, wuq, wuk, wuv, qg, kg, tabs, ntb):
    rows, d = x.shape
    row = _mod_row(ntb)
    mod_spec = pl.BlockSpec((None, 8, d), lambda i: (row(i), 0, 0))
    tab_spec = pl.BlockSpec((TM, LANES), lambda i: (i % ntb, 0))
    y, pmod = prev
    nh = MLA_HEADS * LANES
    consts = [g, w, cqg, ckvg, wuq, wuk, wuv, qg, kg]
    return pl.pallas_call(
        _odd_in_kernel,
        grid=(rows // TM,),
        in_specs=[_tok_spec(d), _tok_spec(d), mod_spec, mod_spec] + [_full_spec(c.shape) for c in consts]
        + [tab_spec, tab_spec, tab_spec],
        out_specs=[_tok_spec(d), _tok_spec(3 * HYENA_WIDTH), _tok_spec(nh), _tok_spec(nh), _tok_spec(nh)],
        out_shape=[jax.ShapeDtypeStruct((rows, d), F32), jax.ShapeDtypeStruct((rows, 3 * HYENA_WIDTH), F32),
                   jax.ShapeDtypeStruct((rows, nh), BF16), jax.ShapeDtypeStruct((rows, nh), BF16),
                   jax.ShapeDtypeStruct((rows, nh), BF16)],
        compiler_params=_params(("parallel",)), name="odd_in",
    )(x, y, pmod, mod, *consts, *tabs)


def _halo_specs(width, ntb):
    per = TM // 8

    def prev_map(i):
        return (jnp.maximum(i * per - 1, 0), 0)

    def next_map(i):
        return ((i + 1) * per - jnp.where(i % ntb == ntb - 1, 1, 0), 0)
    return pl.BlockSpec((8, width), prev_map), pl.BlockSpec((8, width), next_map)


def _halo_valid(i, ntb):
    j = i % ntb
    prev_ok = jnp.logical_and(j != 0, j != ntb - 1)
    next_ok = j < ntb - 2
    return prev_ok, next_ok


def _pool_kernel(a_ref, ap_ref, an_ref, w_ref, sc_ref, o_ref, *, ntb, seq, ctx):
    i = pl.program_id(0)
    prev_ok, next_ok = _halo_valid(i, ntb)
    a = a_ref[...]
    ap = jnp.where(prev_ok, ap_ref[...], 0.0)
    an = jnp.where(next_ok, an_ref[...], 0.0)
    ext = jnp.concatenate([ap, a, an], axis=0)
    rows_ext = TM + 16

    def shifted(d):
        return pltpu.roll(ext, (-d) % rows_ext, 0)[8:8 + TM]

    j = i % ntb
    is_ctx = j == ntb - 1
    pos = jnp.where(is_ctx, 0, j * TM) + lax.broadcasted_iota(jnp.int32, (TM, 1), 0)
    length = jnp.where(is_ctx, ctx, seq)
    lane = lax.broadcasted_iota(jnp.int32, (1, POOL_WIDTH), 1)
    acc = a
    lo, hi = 0, 1
    pooled = jnp.zeros_like(a)
    for g, w in enumerate(POOL_WINDOWS):
        for d in list(range(-w // 2, lo)) + list(range(hi, w // 2)):
            acc = acc + shifted(d)
        lo, hi = -w // 2, w // 2
        cnt = (jnp.minimum(pos + w // 2, length) - jnp.maximum(pos - w // 2, 0)).astype(F32)
        pg = acc / cnt - a
        in_group = jnp.logical_and(lane >= g * POOL_GROUP_DIM, lane < (g + 1) * POOL_GROUP_DIM)
        pooled = jnp.where(in_group, pg, pooled)
    y = _dot(pooled.astype(BF16), w_ref[...]) * sc_ref[...]
    o_ref[...] = y.astype(BF16)


def _pool(a, w_bd, scale, ntb, seq, ctx):
    rows = a.shape[0]
    prev_spec, next_spec = _halo_specs(POOL_WIDTH, ntb)
    return pl.pallas_call(
        functools.partial(_pool_kernel, ntb=ntb, seq=seq, ctx=ctx),
        grid=(rows // TM,),
        in_specs=[_tok_spec(POOL_WIDTH), prev_spec, next_spec, _full_spec(w_bd.shape), _full_spec(scale.shape)],
        out_specs=_tok_spec(POOL_WIDTH),
        out_shape=jax.ShapeDtypeStruct((rows, POOL_WIDTH), BF16),
        compiler_params=_params(("parallel",)), name="pool",
    )(a, a, a, w_bd, scale)


def _finish_heads(acc, extra_den):
    lane = lax.broadcasted_iota(jnp.int32, (1, LANES), 1)
    den = jnp.where(lane < HEAD_DIM, pltpu.roll(acc, HEAD_DIM, 1) + extra_den, 1.0)
    return acc / den


def _sink_col(sink_ref, g, nrow):
    r = lax.broadcasted_iota(jnp.int32, (SWA_GROUP * nrow, 1), 0)
    col = jnp.zeros((SWA_GROUP * nrow, 1), F32)
    for t in range(SWA_GROUP):
        col = jnp.where(r // nrow == t, sink_ref[g * SWA_GROUP + t], col)
    return col


def _swa_lat_kernel(sink_ref, q_ref, kp_ref, kc_ref, kn_ref, kx_ref, vp_ref, vc_ref, vn_ref, vx_ref, o_ref,
                    *, nblk):
    g = pl.program_id(1)
    i = pl.program_id(2)
    q = jnp.concatenate([q_ref[:, LANES * t:LANES * (t + 1)] for t in range(SWA_GROUP)], axis=0)
    kb = jnp.concatenate([kp_ref[...], kc_ref[...], kn_ref[...]], axis=0)
    vb = jnp.concatenate([vp_ref[...], vc_ref[...], vn_ref[...]], axis=0)
    s_b = _dot(q, kb, NT_DIMS)
    s_c = _dot(q, kx_ref[...], NT_DIMS)
    nq = SWA_GROUP * SWA_BLOCK
    qrow = lax.broadcasted_iota(jnp.int32, (nq, 1), 0) % SWA_BLOCK
    kcol = lax.broadcasted_iota(jnp.int32, (1, 3 * SWA_BLOCK), 1)
    valid = jnp.logical_and(kcol >= qrow, kcol <= qrow + 2 * SWA_WINDOW)
    blk = kcol // SWA_BLOCK
    valid = jnp.logical_and(valid, jnp.logical_or(blk != 0, i > 0))
    valid = jnp.logical_and(valid, jnp.logical_or(blk != 2, i < nblk - 1))
    s_b = jnp.where(valid, s_b, -jnp.inf)
    sink = _sink_col(sink_ref, g, SWA_BLOCK)
    m = jnp.maximum(jnp.maximum(jnp.max(s_b, axis=-1, keepdims=True), jnp.max(s_c, axis=-1, keepdims=True)),
                    sink)
    p_b = jnp.exp(s_b - m).astype(BF16)
    p_c = jnp.exp(s_c - m).astype(BF16)
    acc = _dot(p_b, vb) + _dot(p_c, vx_ref[...])
    o = _finish_heads(acc, jnp.exp(sink - m)).astype(BF16)
    for t in range(SWA_GROUP):
        o_ref[:, LANES * t:LANES * (t + 1)] = o[SWA_BLOCK * t:SWA_BLOCK * (t + 1)]


def _swa_ctx_kernel(sink_ref, q_ref, kx_ref, vx_ref, prev_ref, o_ref, *, ctx):
    del prev_ref
    g = pl.program_id(1)
    q = jnp.concatenate([q_ref[:, LANES * t:LANES * (t + 1)] for t in range(SWA_GROUP)], axis=0)
    s = _dot(q, kx_ref[...], NT_DIMS)
    sink = _sink_col(sink_ref, g, ctx)
    m = jnp.maximum(jnp.max(s, axis=-1, keepdims=True), sink)
    p = jnp.exp(s - m).astype(BF16)
    acc = _dot(p, vx_ref[...])
    o = _finish_heads(acc, jnp.exp(sink - m)).astype(BF16)
    for t in range(SWA_GROUP):
        o_ref[:, LANES * t:LANES * (t + 1)] = o[ctx * t:ctx * (t + 1)]


def _swa(q, k, v, sink, nb, seq, ctx):
    rows = q.shape[0]
    n = seq + ctx
    nblk = seq // SWA_BLOCK
    bps = n // SWA_BLOCK
    gw = SWA_GROUP * LANES
    smem = pl.BlockSpec(memory_space=pltpu.SMEM)

    def kv_spec(off):
        return pl.BlockSpec((SWA_BLOCK, LANES),
                            lambda b, g, i: (b * bps + jnp.clip(i + off, 0, nblk - 1), g))
    ctx_spec = pl.BlockSpec((ctx, LANES), lambda b, g, i: (b * (n // ctx) + seq // ctx, g))
    qo_spec = pl.BlockSpec((SWA_BLOCK, gw), lambda b, g, i: (b * bps + i, g))
    out = pl.pallas_call(
        functools.partial(_swa_lat_kernel, nblk=nblk),
        grid=(nb, SWA_KV_HEADS, nblk),
        in_specs=[smem, qo_spec, kv_spec(-1), kv_spec(0), kv_spec(1), ctx_spec,
                  kv_spec(-1), kv_spec(0), kv_spec(1), ctx_spec],
        out_specs=qo_spec,
        out_shape=jax.ShapeDtypeStruct((rows, SWA_Q_HEADS * LANES), BF16),
        compiler_params=_params(("parallel", "parallel", "parallel")), name="swa_latent",
    )(sink, q, k, k, k, k, v, v, v, v)
    cq_spec = pl.BlockSpec((ctx, gw), lambda b, g: (b * (n // ctx) + seq // ctx, g))
    cx_spec = pl.BlockSpec((ctx, LANES), lambda b, g: (b * (n // ctx) + seq // ctx, g))
    return pl.pallas_call(
        functools.partial(_swa_ctx_kernel, ctx=ctx),
        grid=(nb, SWA_KV_HEADS),
        in_specs=[smem, cq_spec, cx_spec, cx_spec, pl.BlockSpec(memory_space=pl.ANY)],
        out_specs=cq_spec,
        out_shape=jax.ShapeDtypeStruct((rows, SWA_Q_HEADS * LANES), BF16),
        input_output_aliases={4: 0},
        compiler_params=_params(("parallel", "parallel")), name="swa_context",
    )(sink, q, k, v, out)


MLA_TQ = 512
MLA_TK = 2048
LOG2E = math.log2(math.e)


def _mla_step(q, kc, vc, m, acc):
    s = _dot(q, kc, NT_DIMS)
    m_new = jnp.maximum(m, jnp.max(s, axis=-1, keepdims=True))
    alpha = jnp.exp2(m - m_new)
    p = jnp.exp2(s - m_new).astype(BF16)
    return m_new, alpha * acc + _dot(p, vc)


def _mla_lat_kernel(q_ref, k_ref, v_ref, o_ref, *, seq, ctx, tk):
    q = q_ref[...]
    tq = q.shape[0]

    def body(c, carry):
        start = pl.multiple_of(c * tk, tk)
        return _mla_step(q, k_ref[pl.ds(start, tk), :], v_ref[pl.ds(start, tk), :], *carry)

    init = (jnp.full((tq, 1), -jnp.inf, F32), jnp.zeros((tq, LANES), F32))
    carry = lax.fori_loop(0, seq // tk, body, init)
    _, acc = _mla_step(q, k_ref[pl.ds(seq, ctx), :], v_ref[pl.ds(seq, ctx), :], *carry)
    o_ref[...] = _finish_heads(acc, 0.0).astype(BF16)


def _mla_ctx_kernel(q_ref, k_ref, v_ref, prev_ref, o_ref):
    del prev_ref
    q = q_ref[...]
    m0 = jnp.full((q.shape[0], 1), -jnp.inf, F32)
    _, acc = _mla_step(q, k_ref[...], v_ref[...], m0, jnp.zeros((q.shape[0], LANES), F32))
    o_ref[...] = _finish_heads(acc, 0.0).astype(BF16)


def _mla(q, k, v, nb, seq, ctx):
    rows = q.shape[0]
    n = seq + ctx
    q3, k3, v3 = (t.reshape(nb, n, MLA_HEADS * LANES) for t in (q, k, v))
    tq = min(MLA_TQ, seq)
    kv_spec = pl.BlockSpec((None, n, LANES), lambda b, h, i: (b, 0, h))
    qo_spec = pl.BlockSpec((None, tq, LANES), lambda b, h, i: (b, i, h))
    out = pl.pallas_call(
        functools.partial(_mla_lat_kernel, seq=seq, ctx=ctx, tk=min(MLA_TK, seq)),
        grid=(nb, MLA_HEADS, seq // tq),
        in_specs=[qo_spec, kv_spec, kv_spec],
        out_specs=qo_spec,
        out_shape=jax.ShapeDtypeStruct((nb, n, MLA_HEADS * LANES), BF16),
        compiler_params=_params(("parallel", "parallel", "parallel")), name="mla_latent",
    )(q3, k3, v3)
    cx_spec = pl.BlockSpec((None, ctx, LANES), lambda b, h: (b, seq // ctx, h))
    out = pl.pallas_call(
        _mla_ctx_kernel,
        grid=(nb, MLA_HEADS),
        in_specs=[cx_spec, cx_spec, cx_spec, pl.BlockSpec(memory_space=pl.ANY)],
        out_specs=cx_spec,
        out_shape=jax.ShapeDtypeStruct((nb, n, MLA_HEADS * LANES), BF16),
        input_output_aliases={3: 0},
        compiler_params=_params(("parallel", "parallel")), name="mla_context",
    )(q3, k3, v3, out)
    return out.reshape(rows, MLA_HEADS * LANES)


def _hy_pre_kernel(u_ref, up_ref, un_ref, cw_ref, cb_ref, skip_ref, x0_ref, zs_ref, z_ref, *, ntb):
    i = pl.program_id(0)
    prev_ok, next_ok = _halo_valid(i, ntb)
    u = u_ref[...]
    r = lax.broadcasted_iota(jnp.int32, (TM, 1), 0)
    up_row = jnp.where(prev_ok, up_ref[7:8, :], 0.0)
    un_row = jnp.where(next_ok, un_ref[0:1, :], 0.0)
    um1 = jnp.where(r == 0, up_row, pltpu.roll(u, 1, 0))
    up1 = jnp.where(r == TM - 1, un_row, pltpu.roll(u, TM - 1, 0))
    uc = um1 * cw_ref[0:1, :] + u * cw_ref[1:2, :] + up1 * cw_ref[2:3, :] + cb_ref[...]
    w = HYENA_WIDTH
    x0 = uc[:, :w]
    z = uc[:, 2 * w:] * uc[:, w:2 * w]
    x0_ref[...] = x0
    zs_ref[...] = x0 * (skip_ref[...] * z)
    z_ref[...] = z.astype(BF16)


def _hy_pre(u, conv_w, conv_b, skip, ntb):
    rows = u.shape[0]
    w3 = 3 * HYENA_WIDTH
    prev_spec, next_spec = _halo_specs(w3, ntb)
    w = HYENA_WIDTH
    return pl.pallas_call(
        functools.partial(_hy_pre_kernel, ntb=ntb),
        grid=(rows // TM,),
        in_specs=[_tok_spec(w3), prev_spec, next_spec, _full_spec(conv_w.shape), _full_spec(conv_b.shape),
                  _full_spec(skip.shape)],
        out_specs=[_tok_spec(w), _tok_spec(w), _tok_spec(w)],
        out_shape=[jax.ShapeDtypeStruct((rows, w), F32), jax.ShapeDtypeStruct((rows, w), F32),
                   jax.ShapeDtypeStruct((rows, w), BF16)],
        compiler_params=_params(("parallel",)), name="hyena_pre",
    )(u, u, u, conv_w, conv_b, skip)


HIGHEST = lax.Precision.HIGHEST


def _filter_kernel(fr_ref, w1_ref, b1_ref, w2_ref, b2_ref, w3_ref, f0_ref, f1_ref, dl_ref, h_ref, n_ref,
                   *, length, tf):
    i = pl.program_id(0)
    pos = (i * tf + lax.broadcasted_iota(jnp.int32, (tf, 1), 0)).astype(F32)
    t = pos / max(length - 1, 1)
    lane = lax.broadcasted_iota(jnp.int32, (1, LANES), 1)
    bands = (HYENA_EMB - 1) // 2
    ang = (2 * math.pi / length) * pos * fr_ref[...]
    emb = jnp.where(lane == 0, t,
                    jnp.where(lane <= bands, jnp.cos(ang), jnp.where(lane <= 2 * bands, -jnp.sin(ang), 0.0)))
    h = jnp.sin(f0_ref[...] * (_dot(emb, w1_ref[...], precision=HIGHEST) + b1_ref[...]))
    h = jnp.sin(f1_ref[...] * (_dot(h, w2_ref[...], precision=HIGHEST) + b2_ref[...]))
    h = _dot(h, w3_ref[...], precision=HIGHEST)
    decay = jnp.exp(-t * dl_ref[...])
    hf = h[:, :HYENA_WIDTH] * decay
    hb = jnp.where(pos == 0.0, 0.0, h[:, HYENA_WIDTH:] * decay)
    h_ref[:, :HYENA_WIDTH] = hf
    h_ref[:, HYENA_WIDTH:] = hb

    @pl.when(i == 0)
    def _():
        n_ref[...] = jnp.zeros_like(n_ref)
    colsum = jnp.concatenate([jnp.sum(jnp.abs(hf), axis=0, keepdims=True),
                              jnp.sum(jnp.abs(hb), axis=0, keepdims=True)], axis=1)
    n_ref[...] += jnp.broadcast_to(colsum, n_ref.shape)


def _hyena_filter(length, fr, w1, b1, w2, b2, w3, f0, f1, deltas):
    tf = min(TM, length)
    consts = [fr, w1, b1, w2, b2, w3, f0, f1, deltas]
    return pl.pallas_call(
        functools.partial(_filter_kernel, length=length, tf=tf),
        grid=(length // tf,),
        in_specs=[_full_spec(c.shape) for c in consts],
        out_specs=[pl.BlockSpec((tf, 2 * HYENA_WIDTH), lambda i: (i, 0)),
                   pl.BlockSpec((8, 2 * HYENA_WIDTH), lambda i: (0, 0))],
        out_shape=[jax.ShapeDtypeStruct((length, 2 * HYENA_WIDTH), F32),
                   jax.ShapeDtypeStruct((8, 2 * HYENA_WIDTH), F32)],
        compiler_params=_params(("arbitrary",)), name="hyena_filter",
    )(*consts)


DFT_TN = 2048
DFT_KG = 8


def _dft1_kernel(f_ref, z_ref, ar_ref, ai_ref):
    a = _dot(f_ref[...], z_ref[...].astype(BF16))
    n1 = ar_ref.shape[0]
    ar_ref[...] = a[:n1].astype(BF16)
    ai_ref[...] = a[n1:].astype(BF16)


def _dft1(f1, z):
    g, _, cols = z.shape
    n1h = f1.shape[1]
    n1 = f1.shape[0] // 2
    tn = min(DFT_TN, cols)
    out_spec = pl.BlockSpec((None, n1, tn), lambda b, j: (b, 0, j))
    return pl.pallas_call(
        _dft1_kernel,
        grid=(g, cols // tn),
        in_specs=[pl.BlockSpec(f1.shape, lambda b, j: (0, 0)),
                  pl.BlockSpec((None, n1h, tn), lambda b, j: (b, 0, j))],
        out_specs=[out_spec, out_spec],
        out_shape=[jax.ShapeDtypeStruct((g, n1, cols), BF16)] * 2,
        compiler_params=_params(("parallel", "parallel")), name="hyena_dft_outer",
    )(f1, z)


def _spec_filter_kernel(m_ref, ar_ref, ai_ref, n_ref, gr_ref, gi_ref, *, inv_n):
    w = HYENA_WIDTH
    nrm = n_ref[0:1, :w] + n_ref[0:1, w:]
    inv = inv_n / nrm
    for kk in range(m_ref.shape[0]):
        x = jnp.concatenate([ar_ref[kk], ai_ref[kk]], axis=0)
        y = _dot(m_ref[kk], x)
        n2 = y.shape[0] // 2
        gr_ref[kk] = (y[:n2, :w] + y[:n2, w:]) * inv
        gi_ref[kk] = (y[n2:, :w] - y[n2:, w:]) * inv


def _spec_kernel(m_ref, ar_ref, ai_ref, gr_ref, gi_ref, cr_ref, ci_ref):
    for kk in range(m_ref.shape[0]):
        mk = m_ref[kk]
        gr, gi = gr_ref[kk], gi_ref[kk]
        for b in range(ar_ref.shape[0]):
            x = jnp.concatenate([ar_ref[b, kk], ai_ref[b, kk]], axis=0)
            y = _dot(mk, x)
            n2 = y.shape[0] // 2
            yr, yi = y[:n2], y[n2:]
            p = jnp.concatenate([yr * gr - yi * gi, yr * gi + yi * gr], axis=0).astype(BF16)
            c = _dot(mk, p, TN_DIMS)
            cr_ref[b, kk] = c[:n2].astype(BF16)
            ci_ref[b, kk] = c[n2:].astype(BF16)


def _spec_filter(mtab, ar, ai, nrm, n_total):
    n1 = mtab.shape[0]
    kg = min(DFT_KG, n1)
    w2 = ar.shape[-1]
    a_spec = pl.BlockSpec((kg, DFT_N2, w2), lambda j: (j, 0, 0))
    g_spec = pl.BlockSpec((kg, DFT_N2, HYENA_WIDTH), lambda j: (j, 0, 0))
    return pl.pallas_call(
        functools.partial(_spec_filter_kernel, inv_n=1.0 / n_total),
        grid=(n1 // kg,),
        in_specs=[pl.BlockSpec((kg, 2 * DFT_N2, 2 * DFT_N2), lambda j: (j, 0, 0)), a_spec, a_spec,
                  _full_spec(nrm.shape)],
        out_specs=[g_spec, g_spec],
        out_shape=[jax.ShapeDtypeStruct((n1, DFT_N2, HYENA_WIDTH), F32)] * 2,
        compiler_params=_params(("parallel",)), name="hyena_filter_spectrum",
    )(mtab, ar, ai, nrm)


def _spec(mtab, ar, ai, gr, gi):
    nb, n1 = ar.shape[0], ar.shape[1]
    kg = min(DFT_KG, n1)
    w = HYENA_WIDTH
    a_spec = pl.BlockSpec((nb, kg, DFT_N2, w), lambda j: (0, j, 0, 0))
    g_spec = pl.BlockSpec((kg, DFT_N2, w), lambda j: (j, 0, 0))
    return pl.pallas_call(
        _spec_kernel,
        grid=(n1 // kg,),
        in_specs=[pl.BlockSpec((kg, 2 * DFT_N2, 2 * DFT_N2), lambda j: (j, 0, 0)), a_spec, a_spec, g_spec, g_spec],
        out_specs=[a_spec, a_spec],
        out_shape=[jax.ShapeDtypeStruct(ar.shape, BF16)] * 2,
        compiler_params=_params(("parallel",)), name="hyena_spectrum",
    )(mtab, ar, ai, gr, gi)


def _idft_kernel(f_ref, cr_ref, ci_ref, x0_ref, zs_ref, o_ref):
    c = jnp.concatenate([cr_ref[...], ci_ref[...]], axis=0)
    y = _dot(f_ref[...], c)
    o_ref[...] = (x0_ref[...] * y + zs_ref[...]).astype(BF16)


def _idft(finv, cr, ci, x0v, zsv):
    nb, n1, cols = cr.shape
    n1h = n1 // 2
    tn = min(DFT_TN, cols)
    c_spec = pl.BlockSpec((None, n1, tn), lambda b, j: (b, 0, j))
    t_spec = pl.BlockSpec((None, n1h, tn), lambda b, j: (b, 0, j))
    return pl.pallas_call(
        _idft_kernel,
        grid=(nb, cols // tn),
        in_specs=[pl.BlockSpec(finv.shape, lambda b, j: (0, 0)), c_spec, c_spec, t_spec, t_spec],
        out_specs=t_spec,
        out_shape=jax.ShapeDtypeStruct(x0v.shape, BF16),
        compiler_params=_params(("parallel", "parallel")), name="hyena_idft_outer",
    )(finv, cr, ci, x0v, zsv)


def _hy_ctx_kernel(fc_ref, fi_ref, z_ref, h_ref, x0_ref, zs_ref, prev_ref, o_ref, *, n_total):
    del prev_ref
    w = HYENA_WIDTH
    h = h_ref[...]
    nk = fc_ref.shape[0] // 2
    nrm = jnp.sum(jnp.abs(h[:, :w]), axis=0, keepdims=True) + jnp.sum(jnp.abs(h[:, w:]), axis=0, keepdims=True)
    inv = (1.0 / n_total) / nrm
    hs = _dot(fc_ref[...], h.astype(BF16))
    gr = (hs[:nk, :w] + hs[:nk, w:]) * inv
    gi = (hs[nk:, :w] - hs[nk:, w:]) * inv
    zsp = _dot(fc_ref[...], z_ref[...])
    zr, zi = zsp[:nk], zsp[nk:]
    p = jnp.concatenate([zr * gr - zi * gi, zr * gi + zi * gr], axis=0).astype(BF16)
    y = _dot(fi_ref[...], p)
    o_ref[...] = (x0_ref[...] * y + zs_ref[...]).astype(BF16)


def _hy_ctx(fc, fi, z, hcat, x0, zs, out, nb, seq, ctx):
    n = seq + ctx
    w = HYENA_WIDTH
    row_spec = pl.BlockSpec((ctx, w), lambda b: (b * (n // ctx) + seq // ctx, 0))
    return pl.pallas_call(
        functools.partial(_hy_ctx_kernel, n_total=2 * ctx),
        grid=(nb,),
        in_specs=[_full_spec(fc.shape), _full_spec(fi.shape), row_spec, _full_spec(hcat.shape), row_spec, row_spec,
                  pl.BlockSpec(memory_space=pl.ANY)],
        out_specs=row_spec,
        out_shape=jax.ShapeDtypeStruct(out.shape, BF16),
        input_output_aliases={6: 0},
        compiler_params=_params(("parallel",)), name="hyena_context",
    )(fc, fi, z, hcat, x0, zs, out)


def _dft_tables(seq, ctx):
    n = 2 * seq
    n1 = n // DFT_N2
    two_pi = 2.0 * math.pi

    def cs(num, den):
        ang = two_pi * (num % den).astype(F32) / den
        return jnp.cos(ang), jnp.sin(ang)
    k1 = jnp.arange(n1, dtype=jnp.int32)
    c, s = cs(k1[:, None] * k1[None, :n1 // 2], n1)
    f1 = jnp.concatenate([c, -s], axis=0).astype(BF16)
    finv = jnp.concatenate([c.T, -s.T], axis=1)
    k2 = jnp.arange(DFT_N2, dtype=jnp.int32)
    phase = k2[None, None, :] * (k1[:, None, None] + n1 * k2[None, :, None])
    c, s = cs(phase, n)
    mtab = jnp.concatenate([jnp.concatenate([c, s], axis=2), jnp.concatenate([-s, c], axis=2)], axis=1)
    nc = 2 * ctx
    kk = jnp.arange(nc, dtype=jnp.int32)
    c, s = cs(kk[:, None] * kk[None, :ctx], nc)
    fc = jnp.concatenate([c, -s], axis=0).astype(BF16)
    fi = jnp.concatenate([c.T, -s.T], axis=1).astype(BF16)
    return f1, finv.astype(BF16), mtab.astype(BF16), fc, fi


N_CLASSES = N_GROUPS * 6
CLS_ROWS = 32
PAIR_LO = (0, 0, 0, 1, 1, 2)
PAIR_HI = (1, 2, 3, 2, 3, 3)
HEXT = LANES


def _post_kernel(a1_ref, a2_ref, w1_ref, w2_ref, x_ref, mod_ref, g_ref, rw_ref, rb_ref,
                 xo_ref, hx_ref, meta_ref, cnt_ref, carry_ref):
    step = pl.program_id(0)

    @pl.when(step == 0)
    def _():
        carry_ref[...] = jnp.zeros_like(carry_ref)
    ml = _dot(a1_ref[...], w1_ref[...]) + _dot(a2_ref[...], w2_ref[...])
    x = x_ref[...] + mod_ref[2:3, :] * ml
    xo_ref[...] = x
    hf = _modulate(x, g_ref[...], mod_ref[3:4, :], mod_ref[4:5, :])
    d = hf.shape[1]
    hx_ref[:, :d] = hf
    h = hf.astype(BF16)
    scores = jax.nn.sigmoid(_dot(rw_ref[...], h, NT_DIMS))
    biased = scores + rb_ref[...]
    sc = [scores[e:e + 1, :] for e in range(N_EXPERTS)]
    bi = [biased[e:e + 1, :] for e in range(N_EXPERTS)]
    best = None
    sel = None
    for g in range(N_GROUPS):
        v0, v1, v2, v3 = bi[4 * g:4 * g + 4]
        top1 = jnp.maximum(jnp.maximum(v0, v1), jnp.maximum(v2, v3))
        top2 = jnp.maximum(jnp.maximum(jnp.minimum(v0, v1), jnp.minimum(v2, v3)),
                           jnp.minimum(jnp.maximum(v0, v1), jnp.maximum(v2, v3)))
        gs = top1 + top2
        if g == 0:
            best, sel = gs, jnp.zeros_like(gs, dtype=jnp.int32)
        else:
            upd = gs > best
            best = jnp.where(upd, gs, best)
            sel = jnp.where(upd, g, sel)

    def pick(vals, j):
        out = vals[j]
        for g in range(1, N_GROUPS):
            out = jnp.where(sel == g, vals[4 * g + j], out)
        return out
    b = [pick(bi, j) for j in range(EXPERTS_PER_GROUP)]
    s = [pick(sc, j) for j in range(EXPERTS_PER_GROUP)]
    i1 = jnp.zeros_like(sel)
    m1 = b[0]
    for j in range(1, EXPERTS_PER_GROUP):
        upd = b[j] > m1
        m1 = jnp.where(upd, b[j], m1)
        i1 = jnp.where(upd, j, i1)
    i2 = jnp.full_like(sel, -1)
    m2 = jnp.full_like(m1, -jnp.inf)
    for j in range(EXPERTS_PER_GROUP):
        upd = jnp.logical_and(i1 != j, b[j] > m2)
        m2 = jnp.where(upd, b[j], m2)
        i2 = jnp.where(upd, j, i2)
    w1 = s[0]
    w2 = s[0]
    for j in range(1, EXPERTS_PER_GROUP):
        w1 = jnp.where(i1 == j, s[j], w1)
        w2 = jnp.where(i2 == j, s[j], w2)
    tot = w1 + w2
    first_lo = i1 < i2
    lo = jnp.minimum(i1, i2)
    hi = jnp.maximum(i1, i2)
    cls = sel * 6 + jnp.where(lo == 0, 0, jnp.where(lo == 1, 3, 5)) + hi - lo - 1
    w_lo = jnp.where(first_lo, w1, w2) / tot
    w_hi = jnp.where(first_lo, w2, w1) / tot
    lrow = lax.broadcasted_iota(jnp.int32, (LANES, 1), 0)
    wt = jnp.where(lrow == 0, w_lo, jnp.where(lrow == 1, w_hi, 0.0))
    hx_ref[:, d:] = wt.T
    tm = cls.shape[1]
    crow = lax.broadcasted_iota(jnp.int32, (CLS_ROWS, 1), 0)
    onehot = crow == cls
    tri = (lax.broadcasted_iota(jnp.int32, (tm, tm), 0) <= lax.broadcasted_iota(jnp.int32, (tm, tm), 1))
    cum = _dot(onehot.astype(BF16), tri.astype(BF16))
    carry = carry_ref[...]
    rank = jnp.sum(jnp.where(onehot, carry[:, 0:1] + cum, 0.0), axis=0, keepdims=True) - 1.0
    mrow = lax.broadcasted_iota(jnp.int32, (8, 1), 0)
    meta_ref[...] = jnp.where(mrow == 0, cls, jnp.where(mrow == 1, rank.astype(jnp.int32), 0))
    carry = carry + cum[:, tm - 1:tm]
    carry_ref[...] = carry
    cnt_ref[...] = carry


def _post(a1, a2, w1, w2, x, mod, g, rw, rb, ntb):
    rows, d = x.shape
    row = _mod_row(ntb)
    mod_spec = pl.BlockSpec((None, 8, d), lambda i: (row(i), 0, 0))
    nt = rows // TM
    return pl.pallas_call(
        _post_kernel,
        grid=(nt,),
        in_specs=[_tok_spec(a1.shape[1]), _tok_spec(a2.shape[1]), _full_spec(w1.shape), _full_spec(w2.shape),
                  _tok_spec(d), mod_spec, _full_spec(g.shape), _full_spec(rw.shape), _full_spec(rb.shape)],
        out_specs=[_tok_spec(d), _tok_spec(d + HEXT), pl.BlockSpec((None, 8, TM), lambda i: (i, 0, 0)),
                   pl.BlockSpec((CLS_ROWS, LANES), lambda i: (0, 0))],
        out_shape=[jax.ShapeDtypeStruct((rows, d), F32), jax.ShapeDtypeStruct((rows, d + HEXT), F32),
                   jax.ShapeDtypeStruct((nt, 8, TM), jnp.int32), jax.ShapeDtypeStruct((CLS_ROWS, LANES), F32)],
        scratch_shapes=[pltpu.VMEM((CLS_ROWS, LANES), F32)],
        compiler_params=_params(("arbitrary",)), name="post_mixer",
    )(a1, a2, w1, w2, x, mod, g, rw, rb)


MOE_TM = 256
MOE_TD = 512


def _route_plan(meta, cnt, rows):
    counts = cnt[:N_CLASSES, 0].astype(jnp.int32)
    padded = ((counts + MOE_TM - 1) // MOE_TM) * MOE_TM
    ends = jnp.cumsum(padded)
    offs = ends - padded
    cls = meta[:, 0, :].reshape(rows)
    rank = meta[:, 1, :].reshape(rows)
    slot = offs[cls] + rank
    ntiles = rows // MOE_TM + N_CLASSES
    tcls = jnp.searchsorted(ends, jnp.arange(ntiles, dtype=jnp.int32) * MOE_TM, side="right").astype(jnp.int32)
    valid = tcls < N_CLASSES
    nvalid = jnp.sum(valid.astype(jnp.int32))
    last = tcls[jnp.maximum(nvalid - 1, 0)]
    tcls = jnp.where(valid, tcls, jnp.minimum(last, N_CLASSES - 1))
    grp, pair = tcls // 6, tcls % 6
    ea = grp * EXPERTS_PER_GROUP + jnp.asarray(PAIR_LO, jnp.int32)[pair]
    eb = grp * EXPERTS_PER_GROUP + jnp.asarray(PAIR_HI, jnp.int32)[pair]
    return slot, ea, eb, valid.astype(jnp.int32), ntiles


def _row_copy(src, dst, sem, s_row, d_row):
    return pltpu.make_async_copy(src.at[pl.ds(s_row, 1), :], dst.at[pl.ds(d_row, 1), :], sem)


def _dispatch_kernel(slot_ref, hx_ref, init_ref, xs_ref, sem):
    del init_ref
    td = hx_ref.shape[0]

    def issue(t, c):
        _row_copy(hx_ref, xs_ref, sem, t, slot_ref[0, t]).start()
        return c
    lax.fori_loop(0, td, issue, 0)
    pltpu.make_async_copy(hx_ref, xs_ref.at[pl.ds(0, td), :], sem).wait()


def _dispatch(hx, slot, nrows_sorted):
    rows, width = hx.shape
    nsteps = rows // MOE_TD
    init = jnp.zeros((nrows_sorted, width), F32)
    return pl.pallas_call(
        _dispatch_kernel,
        grid=(nsteps,),
        in_specs=[pl.BlockSpec((None, 1, MOE_TD), lambda i: (i, 0, 0), memory_space=pltpu.SMEM),
                  pl.BlockSpec((MOE_TD, width), lambda i: (i, 0)),
                  pl.BlockSpec(memory_space=pl.ANY)],
        out_specs=pl.BlockSpec(memory_space=pl.ANY),
        out_shape=jax.ShapeDtypeStruct((nrows_sorted, width), F32),
        scratch_shapes=[pltpu.SemaphoreType.DMA(())],
        input_output_aliases={2: 0},
        compiler_params=_params(("arbitrary",)), name="moe_dispatch",
    )(slot.reshape(nsteps, 1, MOE_TD), hx, init)


def _experts_kernel(ea_ref, eb_ref, valid_ref, xs_ref, wga_ref, wua_ref, wda_ref, wgb_ref, wub_ref, wdb_ref, ys_ref):
    del ea_ref, eb_ref
    j = pl.program_id(0)

    @pl.when(valid_ref[j] == 0)
    def _():
        ys_ref[...] = jnp.zeros_like(ys_ref)

    @pl.when(valid_ref[j] != 0)
    def _():
        d = ys_ref.shape[1]
        x = xs_ref[:, :d].astype(BF16)

        def ffn(wg_ref, wu_ref, wd_ref, w):
            gate = _dot(x, wg_ref[...])
            up = _dot(x, wu_ref[...])
            a = (gate * jax.nn.sigmoid(gate) * up * w).astype(BF16)
            return _dot(a, wd_ref[...])
        ys_ref[...] = (ffn(wga_ref, wua_ref, wda_ref, xs_ref[:, d:d + 1])
                       + ffn(wgb_ref, wub_ref, wdb_ref, xs_ref[:, d + 1:d + 2]))


def _experts(xs, ea, eb, valid, wg, wu, wd, layer, ntiles):
    d = wg.shape[2]
    ff = wg.shape[3]
    gu_a = pl.BlockSpec((None, None, d, ff), lambda j, ea, eb, v: (layer, ea[j], 0, 0))
    gu_b = pl.BlockSpec((None, None, d, ff), lambda j, ea, eb, v: (layer, eb[j], 0, 0))
    dn_a = pl.BlockSpec((None, None, ff, d), lambda j, ea, eb, v: (layer, ea[j], 0, 0))
    dn_b = pl.BlockSpec((None, None, ff, d), lambda j, ea, eb, v: (layer, eb[j], 0, 0))
    return pl.pallas_call(
        _experts_kernel,
        grid_spec=pltpu.PrefetchScalarGridSpec(
            num_scalar_prefetch=3, grid=(ntiles,),
            in_specs=[pl.BlockSpec((MOE_TM, xs.shape[1]), lambda j, ea, eb, v: (j, 0)),
                      gu_a, gu_a, dn_a, gu_b, gu_b, dn_b],
            out_specs=pl.BlockSpec((MOE_TM, d), lambda j, ea, eb, v: (j, 0))),
        out_shape=jax.ShapeDtypeStruct((ntiles * MOE_TM, d), F32),
        compiler_params=_params(("arbitrary",)), name="moe_experts",
    )(ea, eb, valid, xs, wg, wu, wd, wg, wu, wd)


def _undispatch_kernel(slot_ref, ys_ref, y_ref, sem):
    td = y_ref.shape[0]

    def issue(t, c):
        _row_copy(ys_ref, y_ref, sem, slot_ref[0, t], t).start()
        return c
    lax.fori_loop(0, td, issue, 0)
    pltpu.make_async_copy(ys_ref.at[pl.ds(0, td), :], y_ref, sem).wait()


def _undispatch(ys, slot, rows):
    d = ys.shape[1]
    nsteps = rows // MOE_TD
    return pl.pallas_call(
        _undispatch_kernel,
        grid=(nsteps,),
        in_specs=[pl.BlockSpec((None, 1, MOE_TD), lambda i: (i, 0, 0), memory_space=pltpu.SMEM),
                  pl.BlockSpec(memory_space=pl.ANY)],
        out_specs=pl.BlockSpec((MOE_TD, d), lambda i: (i, 0)),
        out_shape=jax.ShapeDtypeStruct((rows, d), F32),
        scratch_shapes=[pltpu.SemaphoreType.DMA(())],
        compiler_params=_params(("arbitrary",)), name="moe_undispatch",
    )(slot.reshape(nsteps, 1, MOE_TD), ys)


def _moe(hx, meta, cnt, wg, wu, wd, layer):
    rows = hx.shape[0]
    slot, ea, eb, valid, ntiles = _route_plan(meta, cnt, rows)
    xs = _dispatch(hx, slot, ntiles * MOE_TM)
    ys = _experts(xs, ea, eb, valid, wg, wu, wd, layer, ntiles)
    return _undispatch(ys, slot, rows)


def _final_kernel(x_ref, y_ref, mod_ref, o_ref):
    o_ref[...] = x_ref[...] + mod_ref[5:6, :] * y_ref[...]


def _final(x, y, mod, nb, seq, ctx):
    d = x.shape[1]
    ntb = (seq + ctx) // TM
    nlt = seq // TM
    tok = pl.BlockSpec((TM, d), lambda b, j: (b * ntb + j, 0))
    return pl.pallas_call(
        _final_kernel,
        grid=(nb, nlt),
        in_specs=[tok, tok, pl.BlockSpec((None, 8, d), lambda b, j: (b, 0, 0))],
        out_specs=pl.BlockSpec((None, TM, d), lambda b, j: (b, j, 0)),
        out_shape=jax.ShapeDtypeStruct((nb, seq, d), F32),
        compiler_params=_params(("parallel", "parallel")), name="final_residual",
    )(x, y, mod)


def _pad_heads(w, heads, dim, axis):
    shp = w.shape
    w = w.reshape(shp[:axis] + (heads, dim) + shp[axis + 1:])
    pad = [(0, 0)] * w.ndim
    pad[axis + 1] = (0, LANES - dim)
    w = jnp.pad(w, pad)
    return w.reshape(shp[:axis] + (heads * LANES,) + shp[axis + 1:])


def _pad_vec(v, mult=1.0):
    return jnp.pad(v.astype(F32) * mult, (0, LANES - v.shape[0])).reshape(1, LANES)


def _rope_tables(seq, ctx, d_rot, off):
    rows = seq // GRID_W
    row = jnp.repeat(jnp.arange(rows), GRID_W).astype(F32)
    col = jnp.tile(jnp.arange(GRID_W), rows).astype(F32)
    n_freq = d_rot // 4
    inv = ROPE_BASE ** (-jnp.arange(n_freq, dtype=F32) / n_freq)
    ang = jnp.concatenate([row[:, None] * inv, col[:, None] * inv], axis=-1)
    cos, sin = jnp.cos(ang), jnp.sin(ang)
    half = d_rot // 2

    def z(r, w):
        return jnp.zeros((r, w), F32)
    rest = LANES - off - 2 * half
    cos_l = jnp.concatenate([jnp.ones((seq, off), F32), cos, cos, z(seq, rest)], axis=1)
    sa_l = jnp.concatenate([z(seq, off + half), sin, z(seq, rest)], axis=1)
    sb_l = jnp.concatenate([z(seq, off), -sin, z(seq, half + rest)], axis=1)
    cos_c = jnp.concatenate([jnp.ones((ctx, off + 2 * half), F32), z(ctx, rest)], axis=1)
    return (jnp.concatenate([cos_l, cos_c], axis=0), jnp.concatenate([sa_l, z(ctx, LANES)], axis=0),
            jnp.concatenate([sb_l, z(ctx, LANES)], axis=0))


def _forward(x, c, ctx, c_ctx, ada_w, ada_b, norm_g, ev_w_in, ev_w_out, pool_w, pool_scale,
             swa_q_gain, swa_k_gain, swa_sink, od_w_in, od_w_out, hy_conv_w, hy_conv_b,
             hy_w1, hy_b1, hy_w2, hy_b2, hy_w3, hy_freq, hy_skip, mla_cq_gain, mla_ckv_gain,
             mla_w_uq, mla_w_ukv, mla_q_gain, mla_k_gain, router_w, router_b,
             moe_w_gate, moe_w_up, moe_w_down):
    nb, seq, d = x.shape
    nctx = ctx.shape[1]
    depth = ada_w.shape[0]
    assert nctx == TM and seq % (2 * TM) == 0 and seq % GRID_W == 0 and nb <= 2
    n = seq + nctx
    ntb = n // TM
    rows = nb * n

    cvec = jnp.concatenate([c, c_ctx[None, :], jnp.zeros((8 - nb - 1, d), F32)], axis=0)
    if nb == 1:
        cvec = jnp.concatenate([c, jnp.zeros((1, d), F32), c_ctx[None, :], jnp.zeros((5, d), F32)], axis=0)
    mod = _adaln(cvec, ada_w, ada_b)

    xs = jnp.concatenate([x, ctx], axis=1).reshape(rows, d)
    tabs_swa = _rope_tables(seq, nctx, HEAD_DIM, 0)
    tabs_mla = _rope_tables(seq, nctx, MLA_ROPE, MLA_NOPE)
    f1, finv, mtab, fc, fi = _dft_tables(seq, nctx)
    bands = (HYENA_EMB - 1) // 2
    frv = jnp.linspace(1e-4, bands - 1, bands, dtype=F32)
    fr = jnp.concatenate([jnp.zeros((1,), F32), frv, frv, jnp.zeros((LANES - 1 - 2 * bands,), F32)]).reshape(1, LANES)
    deltas = jnp.abs(jnp.linspace(math.log(HYENA_TARGET) / HYENA_FAST_DECAY,
                                  math.log(HYENA_TARGET) / HYENA_SLOW_DECAY, HYENA_WIDTH, dtype=F32)).reshape(1, -1)
    rw = jnp.transpose(router_w).astype(BF16)
    rb = router_b.astype(F32).reshape(N_EXPERTS, 1)

    wg16, wu16, wd16 = moe_w_gate.astype(BF16), moe_w_up.astype(BF16), moe_w_down.astype(BF16)

    prev = None
    y = None
    for layer in range(depth):
        i = layer // 2
        lmod = mod[layer]
        g1 = norm_g[layer, 0].reshape(1, d)
        g2 = norm_g[layer, 1].reshape(1, d)
        if layer % 2 == 0:
            w = ev_w_in[i]
            o1 = POOL_WIDTH
            o2 = o1 + SWA_Q_HEADS * HEAD_DIM
            o3 = o2 + SWA_KV_HEADS * HEAD_DIM
            w_in = jnp.concatenate([w[:, :o1], _pad_heads(w[:, o1:o2], SWA_Q_HEADS, HEAD_DIM, 1),
                                    _pad_heads(w[:, o2:o3], SWA_KV_HEADS, HEAD_DIM, 1),
                                    _pad_heads(w[:, o3:], SWA_KV_HEADS, HEAD_DIM, 1)], axis=1).astype(BF16)
            qg = _pad_vec(swa_q_gain[i], HEAD_DIM ** -0.5)
            kg = _pad_vec(swa_k_gain[i])
            xs, a, q, k, v = _even_in(xs, prev, lmod, g1, w_in, qg, kg, tabs_swa, ntb)
            w_bd = jnp.zeros((POOL_WIDTH, POOL_WIDTH), F32)
            for g in range(POOL_GROUPS):
                sl = slice(g * POOL_GROUP_DIM, (g + 1) * POOL_GROUP_DIM)
                w_bd = w_bd.at[sl, sl].set(pool_w[i, g])
            mix1 = _pool(a, w_bd.astype(BF16), pool_scale[i].reshape(1, -1), ntb, seq, nctx)
            mix2 = _swa(q, k, v, swa_sink[i].astype(F32), nb, seq, nctx)
            wo = ev_w_out[i]
            wo1 = wo[:POOL_WIDTH].astype(BF16)
            wo2 = _pad_heads(wo[POOL_WIDTH:], SWA_Q_HEADS, HEAD_DIM, 0).astype(BF16)
        else:
            w_in = jnp.pad(od_w_in[i], ((0, 0), (0, LANES - MLA_ROPE))).astype(BF16)
            wuq = _pad_heads(mla_w_uq[i], MLA_HEADS, MLA_QK, 1).astype(BF16)
            wukv = mla_w_ukv[i].reshape(MLA_KV_RANK, MLA_HEADS, MLA_NOPE + MLA_V)
            wuk = _pad_heads(wukv[:, :, :MLA_NOPE].reshape(MLA_KV_RANK, -1), MLA_HEADS, MLA_NOPE, 1).astype(BF16)
            wuv = _pad_heads(wukv[:, :, MLA_NOPE:].reshape(MLA_KV_RANK, -1), MLA_HEADS, MLA_V, 1).astype(BF16)
            qg = _pad_vec(mla_q_gain[i], MLA_QK ** -0.5 * LOG2E)
            kg = _pad_vec(mla_k_gain[i])
            xs, u, q, k, v = _odd_in(xs, prev, lmod, g1, w_in, mla_cq_gain[i].reshape(1, -1),
                                     mla_ckv_gain[i].reshape(1, -1), wuq, wuk, wuv, qg, kg, tabs_mla, ntb)
            x0, zs, z = _hy_pre(u, hy_conv_w[i], hy_conv_b[i].reshape(1, -1), hy_skip[i].reshape(1, -1), ntb)
            w1p = jnp.zeros((LANES, LANES), F32).at[:HYENA_EMB, :HYENA_HIDDEN].set(hy_w1[i])
            w2p = jnp.zeros((LANES, LANES), F32).at[:HYENA_HIDDEN, :HYENA_HIDDEN].set(hy_w2[i])
            w3p = jnp.zeros((LANES, 2 * HYENA_WIDTH), F32).at[:HYENA_HIDDEN].set(hy_w3[i])
            fparams = (fr, w1p, _pad_vec(hy_b1[i]), w2p, _pad_vec(hy_b2[i]), w3p,
                       _pad_vec(hy_freq[i, 0]), _pad_vec(hy_freq[i, 1]), deltas)
            hcat, nrm = _hyena_filter(seq, *fparams)
            hcat_c, _ = _hyena_filter(nctx, *fparams)
            n1 = 2 * seq // DFT_N2
            w = HYENA_WIDTH
            har, hai = _dft1(f1, hcat.astype(BF16).reshape(1, n1 // 2, DFT_N2 * 2 * w))
            gr, gi = _spec_filter(mtab, har.reshape(n1, DFT_N2, 2 * w), hai.reshape(n1, DFT_N2, 2 * w),
                                  nrm, 2 * seq)
            ar, ai = _dft1(f1, z.reshape(nb, n // DFT_N2, DFT_N2 * w))
            cr, ci = _spec(mtab, ar.reshape(nb, n1, DFT_N2, w), ai.reshape(nb, n1, DFT_N2, w), gr, gi)
            view = (nb, n // DFT_N2, DFT_N2 * w)
            hy = _idft(finv, cr.reshape(nb, n1, DFT_N2 * w), ci.reshape(nb, n1, DFT_N2 * w),
                       x0.reshape(view), zs.reshape(view))
            mix1 = _hy_ctx(fc, fi, z, hcat_c, x0, zs, hy.reshape(rows, w), nb, seq, nctx)
            mix2 = _mla(q, k, v, nb, seq, nctx)
            wo = od_w_out[i]
            wo1 = wo[:HYENA_WIDTH].astype(BF16)
            wo2 = _pad_heads(wo[HYENA_WIDTH:], MLA_HEADS, MLA_V, 0).astype(BF16)
        xs, hx, meta, cnt = _post(mix1, mix2, wo1, wo2, xs, lmod, g2, rw, rb, ntb)
        y = _moe(hx, meta, cnt, wg16, wu16, wd16, layer)
        prev = (y, lmod)
    return _final(xs, y, mod[depth - 1], nb, seq, nctx)


def kernel(x, c, ctx, c_ctx, ada_w, ada_b, norm_g, ev_w_in, ev_w_out, pool_w, pool_scale, swa_q_gain, swa_k_gain, swa_sink, od_w_in, od_w_out, hy_conv_w, hy_conv_b, hy_w1, hy_b1, hy_w2, hy_b2, hy_w3, hy_freq, hy_skip, mla_cq_gain, mla_ckv_gain, mla_w_uq, mla_w_ukv, mla_q_gain, mla_k_gain, router_w, router_b, moe_w_gate, moe_w_up, moe_w_down):
    return _forward(x, c, ctx, c_ctx, ada_w, ada_b, norm_g, ev_w_in, ev_w_out, pool_w, pool_scale,
                    swa_q_gain, swa_k_gain, swa_sink, od_w_in, od_w_out, hy_conv_w, hy_conv_b,
                    hy_w1, hy_b1, hy_w2, hy_b2, hy_w3, hy_freq, hy_skip, mla_cq_gain, mla_ckv_gain,
                    mla_w_uq, mla_w_ukv, mla_q_gain, mla_k_gain, router_w, router_b,
                    moe_w_gate, moe_w_up, moe_w_down)
```

```python
import functools
import math

import jax
import jax.numpy as jnp
from jax import lax
from jax.experimental import pallas as pl
from jax.experimental.pallas import tpu as pltpu

F32 = jnp.float32
BF16 = jnp.bfloat16

GRID_W = 64
HEAD_DIM = 64
ROPE_BASE = 10000.0
EPS = 1e-6
POOL_GROUPS = 4
POOL_GROUP_DIM = 64
POOL_WIDTH = POOL_GROUPS * POOL_GROUP_DIM
POOL_WINDOWS = (2, 4, 8, 16)
SWA_Q_HEADS = 12
SWA_KV_HEADS = 4
SWA_GROUP = SWA_Q_HEADS // SWA_KV_HEADS
SWA_WINDOW = 128
SWA_BLOCK = 128
HYENA_WIDTH = 512
HYENA_EMB = 33
HYENA_HIDDEN = 64
HYENA_FAST_DECAY = 0.3
HYENA_SLOW_DECAY = 1.5
HYENA_TARGET = 1e-2
MLA_HEADS = 8
MLA_NOPE = 64
MLA_ROPE = 32
MLA_QK = MLA_NOPE + MLA_ROPE
MLA_V = 64
MLA_Q_RANK = 256
MLA_KV_RANK = 128
N_EXPERTS = 16
N_GROUPS = 4
EXPERTS_PER_GROUP = N_EXPERTS // N_GROUPS
EXPERT_FF = 512

LANES = 128
TM = 256
DFT_N2 = 128
VMEM_LIMIT = 56 * 1024 * 1024

NT_DIMS = (((1,), (1,)), ((), ()))
TN_DIMS = (((0,), (0,)), ((), ()))


def _dot(a, b, dims=None, precision=None):
    if dims is None:
        return jnp.dot(a, b, preferred_element_type=F32, precision=precision)
    return lax.dot_general(a, b, dims, preferred_element_type=F32, precision=precision)


def _params(sem):
    return pltpu.CompilerParams(dimension_semantics=sem, vmem_limit_bytes=VMEM_LIMIT)


def _mod_row(ntb):
    def f(i):
        return jnp.where(i % ntb == ntb - 1, 2, i // ntb)
    return f


def _modulate(x, g, shift, scale):
    ms = jnp.mean(x * x, axis=-1, keepdims=True)
    return (x * lax.rsqrt(ms + EPS) * g) * (1.0 + scale) + shift


def _head_norm_rope(xh, real_dim, gain, cos, sa, sb, half):
    r = lax.rsqrt(jnp.sum(xh * xh, axis=-1, keepdims=True) * (1.0 / real_dim) + EPS)
    xn = xh * r * gain
    return xn * cos + pltpu.roll(xn, half, 1) * sa + pltpu.roll(xn, LANES - half, 1) * sb


def _adaln_kernel(c_ref, w_ref, b_ref, o_ref):
    c = c_ref[...]
    s = (c * jax.nn.sigmoid(c)).astype(BF16)
    o_ref[...] = _dot(s, w_ref[...].astype(BF16)) + b_ref[...]


def _adaln(cvec, ada_w, ada_b):
    depth, d, d6 = ada_w.shape
    nchunk = d6 // d
    out = pl.pallas_call(
        _adaln_kernel,
        grid=(depth, nchunk),
        in_specs=[pl.BlockSpec((8, d), lambda l, j: (0, 0)),
                  pl.BlockSpec((None, d, d), lambda l, j: (l, 0, j)),
                  pl.BlockSpec((None, 1, d), lambda l, j: (l, 0, j))],
        out_specs=pl.BlockSpec((None, None, 8, d), lambda l, j: (l, j, 0, 0)),
        out_shape=jax.ShapeDtypeStruct((depth, nchunk, 8, d), F32),
        compiler_params=_params(("parallel", "parallel")),
        name="adaln",
    )(cvec, ada_w, ada_b.reshape(depth, 1, d6))
    mod = jnp.transpose(out, (0, 2, 1, 3))
    return jnp.pad(mod, ((0, 0), (0, 0), (0, 8 - nchunk), (0, 0)))


def _even_in_kernel(*refs, has_prev):
    if has_prev:
        (x_ref, y_ref, pmod_ref, mod_ref, g_ref, w_ref, qg_ref, kg_ref, cos_ref, sa_ref, sb_ref,
         xo_ref, a_ref, q_ref, k_ref, v_ref) = refs
        x = x_ref[...] + pmod_ref[5:6, :] * y_ref[...]
        xo_ref[...] = x
    else:
        (x_ref, mod_ref, g_ref, w_ref, qg_ref, kg_ref, cos_ref, sa_ref, sb_ref,
         a_ref, q_ref, k_ref, v_ref) = refs
        x = x_ref[...]
    h = _modulate(x, g_ref[...], mod_ref[0:1, :], mod_ref[1:2, :]).astype(BF16)
    p = _dot(h, w_ref[...])
    a_ref[...] = p[:, :POOL_WIDTH]
    cos, sa, sb = cos_ref[...], sa_ref[...], sb_ref[...]
    o = POOL_WIDTH
    for hh in range(SWA_Q_HEADS):
        xh = p[:, o + LANES * hh:o + LANES * (hh + 1)]
        q_ref[:, LANES * hh:LANES * (hh + 1)] = _head_norm_rope(
            xh, HEAD_DIM, qg_ref[...], cos, sa, sb, HEAD_DIM // 2).astype(BF16)
    o += SWA_Q_HEADS * LANES
    for hh in range(SWA_KV_HEADS):
        xh = p[:, o + LANES * hh:o + LANES * (hh + 1)]
        k_ref[:, LANES * hh:LANES * (hh + 1)] = _head_norm_rope(
            xh, HEAD_DIM, kg_ref[...], cos, sa, sb, HEAD_DIM // 2).astype(BF16)
    o += SWA_KV_HEADS * LANES
    ones_hi = (lax.broadcasted_iota(jnp.int32, (1, LANES), 1) >= HEAD_DIM).astype(F32)
    for hh in range(SWA_KV_HEADS):
        vh = p[:, o + LANES * hh:o + LANES * (hh + 1)]
        v_ref[:, LANES * hh:LANES * (hh + 1)] = (vh + ones_hi).astype(BF16)


def _odd_in_kernel(x_ref, y_ref, pmod_ref, mod_ref, g_ref, w_ref, cqg_ref, ckvg_ref, wuq_ref, wuk_ref,
                   wuv_ref, qg_ref, kg_ref, cos_ref, sa_ref, sb_ref,
                   xo_ref, u_ref, q_ref, k_ref, v_ref):
    x = x_ref[...] + pmod_ref[5:6, :] * y_ref[...]
    xo_ref[...] = x
    h = _modulate(x, g_ref[...], mod_ref[0:1, :], mod_ref[1:2, :]).astype(BF16)
    p = _dot(h, w_ref[...])
    nu = 3 * HYENA_WIDTH
    u_ref[...] = p[:, :nu]
    cq = p[:, nu:nu + MLA_Q_RANK]
    ckv = p[:, nu + MLA_Q_RANK:nu + MLA_Q_RANK + MLA_KV_RANK]
    krb = p[:, nu + MLA_Q_RANK + MLA_KV_RANK:]
    cqn = (cq * lax.rsqrt(jnp.mean(cq * cq, axis=-1, keepdims=True) + EPS) * cqg_ref[...]).astype(BF16)
    ckvn = (ckv * lax.rsqrt(jnp.mean(ckv * ckv, axis=-1, keepdims=True) + EPS) * ckvg_ref[...]).astype(BF16)
    qp = _dot(cqn, wuq_ref[...])
    kp = _dot(ckvn, wuk_ref[...])
    vp = _dot(ckvn, wuv_ref[...])
    krp = pltpu.roll(krb, MLA_NOPE, 1)
    cos, sa, sb = cos_ref[...], sa_ref[...], sb_ref[...]
    ones_hi = (lax.broadcasted_iota(jnp.int32, (1, LANES), 1) >= MLA_V).astype(F32)
    for hh in range(MLA_HEADS):
        sl = slice(LANES * hh, LANES * (hh + 1))
        q_ref[:, sl] = _head_norm_rope(qp[:, sl], MLA_QK, qg_ref[...], cos, sa, sb,
                                       MLA_ROPE // 2).astype(BF16)
        k_ref[:, sl] = _head_norm_rope(kp[:, sl] + krp, MLA_QK, kg_ref[...], cos, sa, sb,
                                       MLA_ROPE // 2).astype(BF16)
        v_ref[:, sl] = (vp[:, sl] + ones_hi).astype(BF16)


def _tok_spec(width):
    return pl.BlockSpec((TM, width), lambda i: (i, 0))


def _full_spec(shape):
    nd = len(shape)
    return pl.BlockSpec(shape, lambda i: (0,) * nd)


def _even_in(x, prev, mod, g, w, qg, kg, tabs, ntb):
    rows, d = x.shape
    row = _mod_row(ntb)
    mod_spec = pl.BlockSpec((None, 8, d), lambda i: (row(i), 0, 0))
    tab_spec = pl.BlockSpec((TM, LANES), lambda i: (i % ntb, 0))
    nq, nk = SWA_Q_HEADS * LANES, SWA_KV_HEADS * LANES
    ins = [x]
    specs = [_tok_spec(d)]
    outs = []
    ospecs = []
    if prev is not None:
        y, pmod = prev
        ins += [y, pmod]
        specs += [_tok_spec(d), mod_spec]
        outs.append(jax.ShapeDtypeStruct((rows, d), F32))
        ospecs.append(_tok_spec(d))
    ins += [mod, g, w, qg, kg, *tabs]
    specs += [mod_spec, _full_spec(g.shape), _full_spec(w.shape), _full_spec(qg.shape), _full_spec(kg.shape),
              tab_spec, tab_spec, tab_spec]
    outs += [jax.ShapeDtypeStruct((rows, POOL_WIDTH), F32), jax.ShapeDtypeStruct((rows, nq), BF16),
             jax.ShapeDtypeStruct((rows, nk), BF16), jax.ShapeDtypeStruct((rows, nk), BF16)]
    ospecs += [_tok_spec(POOL_WIDTH), _tok_spec(nq), _tok_spec(nk), _tok_spec(nk)]
    res = pl.pallas_call(
        functools.partial(_even_in_kernel, has_prev=prev is not None),
        grid=(rows // TM,), in_specs=specs, out_specs=ospecs, out_shape=outs,
        compiler_params=_params(("parallel",)), name="even_in",
    )(*ins)
    if prev is None:
        return (x, *res)
    return res


def _odd_in(x, prev, mod, g, w, cqg, ckvg, wuq, wuk, wuv, qg, kg, tabs, ntb):
    rows, d = x.shape
    row = _mod_row(ntb)
    mod_spec = pl.BlockSpec((None, 8, d), lambda i: (row(i), 0, 0))
    tab_spec = pl.BlockSpec((TM, LANES), lambda i: (i % ntb, 0))
    y, pmod = prev
    nh = MLA_HEADS * LANES
    consts = [g, w, cqg, ckvg, wuq, wuk, wuv, qg, kg]
    return pl.pallas_call(
        _odd_in_kernel,
        grid=(rows // TM,),
        in_specs=[_tok_spec(d), _tok_spec(d), mod_spec, mod_spec] + [_full_spec(c.shape) for c in consts]
        + [tab_spec, tab_spec, tab_spec],
        out_specs=[_tok_spec(d), _tok_spec(3 * HYENA_WIDTH), _tok_spec(nh), _tok_spec(nh), _tok_spec(nh)],
        out_shape=[jax.ShapeDtypeStruct((rows, d), F32), jax.ShapeDtypeStruct((rows, 3 * HYENA_WIDTH), F32),
                   jax.ShapeDtypeStruct((rows, nh), BF16), jax.ShapeDtypeStruct((rows, nh), BF16),
                   jax.ShapeDtypeStruct((rows, nh), BF16)],
        compiler_params=_params(("parallel",)), name="odd_in",
    )(x, y, pmod, mod, *consts, *tabs)


def _halo_specs(width, ntb):
    per = TM // 8

    def prev_map(i):
        return (jnp.maximum(i * per - 1, 0), 0)

    def next_map(i):
        return ((i + 1) * per - jnp.where(i % ntb == ntb - 1, 1, 0), 0)
    return pl.BlockSpec((8, width), prev_map), pl.BlockSpec((8, width), next_map)


def _halo_valid(i, ntb):
    j = i % ntb
    prev_ok = jnp.logical_and(j != 0, j != ntb - 1)
    next_ok = j < ntb - 2
    return prev_ok, next_ok


def _pool_kernel(a_ref, ap_ref, an_ref, w_ref, sc_ref, o_ref, *, ntb, seq, ctx):
    i = pl.program_id(0)
    prev_ok, next_ok = _halo_valid(i, ntb)
    a = a_ref[...]
    ap = jnp.where(prev_ok, ap_ref[...], 0.0)
    an = jnp.where(next_ok, an_ref[...], 0.0)
    ext = jnp.concatenate([ap, a, an], axis=0)
    rows_ext = TM + 16

    def shifted(d):
        return pltpu.roll(ext, (-d) % rows_ext, 0)[8:8 + TM]

    j = i % ntb
    is_ctx = j == ntb - 1
    pos = jnp.where(is_ctx, 0, j * TM) + lax.broadcasted_iota(jnp.int32, (TM, 1), 0)
    length = jnp.where(is_ctx, ctx, seq)
    lane = lax.broadcasted_iota(jnp.int32, (1, POOL_WIDTH), 1)
    acc = a
    lo, hi = 0, 1
    pooled = jnp.zeros_like(a)
    for g, w in enumerate(POOL_WINDOWS):
        for d in list(range(-w // 2, lo)) + list(range(hi, w // 2)):
            acc = acc + shifted(d)
        lo, hi = -w // 2, w // 2
        cnt = (jnp.minimum(pos + w // 2, length) - jnp.maximum(pos - w // 2, 0)).astype(F32)
        pg = acc / cnt - a
        in_group = jnp.logical_and(lane >= g * POOL_GROUP_DIM, lane < (g + 1) * POOL_GROUP_DIM)
        pooled = jnp.where(in_group, pg, pooled)
    y = _dot(pooled.astype(BF16), w_ref[...]) * sc_ref[...]
    o_ref[...] = y.astype(BF16)


def _pool(a, w_bd, scale, ntb, seq, ctx):
    rows = a.shape[0]
    prev_spec, next_spec = _halo_specs(POOL_WIDTH, ntb)
    return pl.pallas_call(
        functools.partial(_pool_kernel, ntb=ntb, seq=seq, ctx=ctx),
        grid=(rows // TM,),
        in_specs=[_tok_spec(POOL_WIDTH), prev_spec, next_spec, _full_spec(w_bd.shape), _full_spec(scale.shape)],
        out_specs=_tok_spec(POOL_WIDTH),
        out_shape=jax.ShapeDtypeStruct((rows, POOL_WIDTH), BF16),
        compiler_params=_params(("parallel",)), name="pool",
    )(a, a, a, w_bd, scale)


def _finish_heads(acc, extra_den):
    lane = lax.broadcasted_iota(jnp.int32, (1, LANES), 1)
    den = jnp.where(lane < HEAD_DIM, pltpu.roll(acc, HEAD_DIM, 1) + extra_den, 1.0)
    return acc / den


def _sink_col(sink_ref, g, nrow):
    r = lax.broadcasted_iota(jnp.int32, (SWA_GROUP * nrow, 1), 0)
    col = jnp.zeros((SWA_GROUP * nrow, 1), F32)
    for t in range(SWA_GROUP):
        col = jnp.where(r // nrow == t, sink_ref[g * SWA_GROUP + t], col)
    return col


def _swa_kernel(sink_ref, q_ref, kp_ref, kc_ref, kn_ref, kx_ref, vp_ref, vc_ref, vn_ref, vx_ref, o_ref, *, nblk):
    i = pl.program_id(1)
    nq = SWA_GROUP * SWA_BLOCK
    qrow = lax.broadcasted_iota(jnp.int32, (nq, 1), 0) % SWA_BLOCK
    kcol = lax.broadcasted_iota(jnp.int32, (1, 3 * SWA_BLOCK), 1)
    valid = jnp.logical_and(kcol >= qrow, kcol <= qrow + 2 * SWA_WINDOW)
    blk = kcol // SWA_BLOCK
    valid = jnp.logical_and(valid, jnp.logical_or(blk != 0, i > 0))
    valid = jnp.logical_and(valid, jnp.logical_or(blk != 2, i < nblk - 1))
    valid = jnp.logical_and(valid, i < nblk)
    for g in range(SWA_KV_HEADS):
        gs = slice(LANES * g, LANES * (g + 1))
        qo = LANES * SWA_GROUP * g
        q = jnp.concatenate([q_ref[:, qo + LANES * t:qo + LANES * (t + 1)] for t in range(SWA_GROUP)], axis=0)
        kb = jnp.concatenate([kp_ref[:, gs], kc_ref[:, gs], kn_ref[:, gs]], axis=0)
        vb = jnp.concatenate([vp_ref[:, gs], vc_ref[:, gs], vn_ref[:, gs]], axis=0)
        s_b = jnp.where(valid, _dot(q, kb, NT_DIMS), -jnp.inf)
        s_c = _dot(q, kx_ref[:, gs], NT_DIMS)
        sink = _sink_col(sink_ref, g, SWA_BLOCK)
        m = jnp.maximum(jnp.maximum(jnp.max(s_b, axis=-1, keepdims=True), jnp.max(s_c, axis=-1, keepdims=True)),
                        sink)
        p_b = jnp.exp(s_b - m).astype(BF16)
        p_c = jnp.exp(s_c - m).astype(BF16)
        acc = _dot(p_b, vb) + _dot(p_c, vx_ref[:, gs])
        o = _finish_heads(acc, jnp.exp(sink - m)).astype(BF16)
        for t in range(SWA_GROUP):
            o_ref[:, qo + LANES * t:qo + LANES * (t + 1)] = o[SWA_BLOCK * t:SWA_BLOCK * (t + 1)]


def _swa(q, k, v, sink, nb, seq, ctx):
    rows = q.shape[0]
    n = seq + ctx
    nblk = seq // SWA_BLOCK
    bps = n // SWA_BLOCK
    kw = SWA_KV_HEADS * LANES
    qw = SWA_Q_HEADS * LANES
    smem = pl.BlockSpec(memory_space=pltpu.SMEM)

    def kv_spec(off):
        return pl.BlockSpec((SWA_BLOCK, kw), lambda b, i: (b * bps + jnp.clip(i + off, 0, nblk - 1), 0))
    ctx_spec = pl.BlockSpec((ctx, kw), lambda b, i: (b * (n // ctx) + seq // ctx, 0))
    qo_spec = pl.BlockSpec((SWA_BLOCK, qw), lambda b, i: (b * bps + i, 0))
    return pl.pallas_call(
        functools.partial(_swa_kernel, nblk=nblk),
        grid=(nb, bps),
        in_specs=[smem, qo_spec, kv_spec(-1), kv_spec(0), kv_spec(1), ctx_spec,
                  kv_spec(-1), kv_spec(0), kv_spec(1), ctx_spec],
        out_specs=qo_spec,
        out_shape=jax.ShapeDtypeStruct((rows, qw), BF16),
        compiler_params=_params(("parallel", "parallel")), name="swa",
    )(sink, q, k, k, k, k, v, v, v, v)


MLA_TQ = 512
MLA_TK = 2048
LOG2E = math.log2(math.e)


def _mla_step(q, kc, vc, m, acc):
    s = _dot(q, kc, NT_DIMS)
    m_new = jnp.maximum(m, jnp.max(s, axis=-1, keepdims=True))
    alpha = jnp.exp2(m - m_new)
    p = jnp.exp2(s - m_new).astype(BF16)
    return m_new, alpha * acc + _dot(p, vc)


def _mla_lat_kernel(q_ref, k_ref, v_ref, o_ref, *, seq, ctx, tk):
    q = q_ref[...]
    tq = q.shape[0]

    def body(c, carry):
        start = pl.multiple_of(c * tk, tk)
        return _mla_step(q, k_ref[pl.ds(start, tk), :], v_ref[pl.ds(start, tk), :], *carry)

    init = (jnp.full((tq, 1), -jnp.inf, F32), jnp.zeros((tq, LANES), F32))
    carry = lax.fori_loop(0, seq // tk, body, init)
    _, acc = _mla_step(q, k_ref[pl.ds(seq, ctx), :], v_ref[pl.ds(seq, ctx), :], *carry)
    o_ref[...] = _finish_heads(acc, 0.0).astype(BF16)


def _mla_ctx_kernel(q_ref, k_ref, v_ref, prev_ref, o_ref):
    del prev_ref
    q = q_ref[...]
    m0 = jnp.full((q.shape[0], 1), -jnp.inf, F32)
    _, acc = _mla_step(q, k_ref[...], v_ref[...], m0, jnp.zeros((q.shape[0], LANES), F32))
    o_ref[...] = _finish_heads(acc, 0.0).astype(BF16)


def _mla(q, k, v, nb, seq, ctx):
    rows = q.shape[0]
    n = seq + ctx
    q3, k3, v3 = (t.reshape(nb, n, MLA_HEADS * LANES) for t in (q, k, v))
    tq = min(MLA_TQ, seq)
    kv_spec = pl.BlockSpec((None, n, LANES), lambda b, h, i: (b, 0, h))
    qo_spec = pl.BlockSpec((None, tq, LANES), lambda b, h, i: (b, i, h))
    out = pl.pallas_call(
        functools.partial(_mla_lat_kernel, seq=seq, ctx=ctx, tk=min(MLA_TK, seq)),
        grid=(nb, MLA_HEADS, seq // tq),
        in_specs=[qo_spec, kv_spec, kv_spec],
        out_specs=qo_spec,
        out_shape=jax.ShapeDtypeStruct((nb, n, MLA_HEADS * LANES), BF16),
        compiler_params=_params(("parallel", "parallel", "parallel")), name="mla_latent",
    )(q3, k3, v3)
    cx_spec = pl.BlockSpec((None, ctx, LANES), lambda b, h: (b, seq // ctx, h))
    out = pl.pallas_call(
        _mla_ctx_kernel,
        grid=(nb, MLA_HEADS),
        in_specs=[cx_spec, cx_spec, cx_spec, pl.BlockSpec(memory_space=pl.ANY)],
        out_specs=cx_spec,
        out_shape=jax.ShapeDtypeStruct((nb, n, MLA_HEADS * LANES), BF16),
        input_output_aliases={3: 0},
        compiler_params=_params(("parallel", "parallel")), name="mla_context",
    )(q3, k3, v3, out)
    return out.reshape(rows, MLA_HEADS * LANES)


def _hy_pre_kernel(u_ref, up_ref, un_ref, cw_ref, cb_ref, skip_ref, x0_ref, zs_ref, z_ref, *, ntb):
    i = pl.program_id(0)
    prev_ok, next_ok = _halo_valid(i, ntb)
    u = u_ref[...]
    r = lax.broadcasted_iota(jnp.int32, (TM, 1), 0)
    up_row = jnp.where(prev_ok, up_ref[7:8, :], 0.0)
    un_row = jnp.where(next_ok, un_ref[0:1, :], 0.0)
    um1 = jnp.where(r == 0, up_row, pltpu.roll(u, 1, 0))
    up1 = jnp.where(r == TM - 1, un_row, pltpu.roll(u, TM - 1, 0))
    uc = um1 * cw_ref[0:1, :] + u * cw_ref[1:2, :] + up1 * cw_ref[2:3, :] + cb_ref[...]
    w = HYENA_WIDTH
    x0 = uc[:, :w]
    z = uc[:, 2 * w:] * uc[:, w:2 * w]
    x0_ref[...] = x0
    zs_ref[...] = x0 * (skip_ref[...] * z)
    z_ref[...] = z


def _hy_pre(u, conv_w, conv_b, skip, ntb):
    rows = u.shape[0]
    w3 = 3 * HYENA_WIDTH
    prev_spec, next_spec = _halo_specs(w3, ntb)
    w = HYENA_WIDTH
    return pl.pallas_call(
        functools.partial(_hy_pre_kernel, ntb=ntb),
        grid=(rows // TM,),
        in_specs=[_tok_spec(w3), prev_spec, next_spec, _full_spec(conv_w.shape), _full_spec(conv_b.shape),
                  _full_spec(skip.shape)],
        out_specs=[_tok_spec(w), _tok_spec(w), _tok_spec(w)],
        out_shape=[jax.ShapeDtypeStruct((rows, w), F32)] * 3,
        compiler_params=_params(("parallel",)), name="hyena_pre",
    )(u, u, u, conv_w, conv_b, skip)


HIGHEST = lax.Precision.HIGHEST


def _filter_kernel(fr_ref, w1_ref, b1_ref, w2_ref, b2_ref, w3_ref, f0_ref, f1_ref, dl_ref, h_ref, n_ref,
                   *, length, tf):
    i = pl.program_id(0)
    pos = (i * tf + lax.broadcasted_iota(jnp.int32, (tf, 1), 0)).astype(F32)
    t = pos / max(length - 1, 1)
    lane = lax.broadcasted_iota(jnp.int32, (1, LANES), 1)
    bands = (HYENA_EMB - 1) // 2
    ang = (2 * math.pi / length) * pos * fr_ref[...]
    emb = jnp.where(lane == 0, t,
                    jnp.where(lane <= bands, jnp.cos(ang), jnp.where(lane <= 2 * bands, -jnp.sin(ang), 0.0)))
    h = jnp.sin(f0_ref[...] * (_dot(emb, w1_ref[...], precision=HIGHEST) + b1_ref[...]))
    h = jnp.sin(f1_ref[...] * (_dot(h, w2_ref[...], precision=HIGHEST) + b2_ref[...]))
    h = _dot(h, w3_ref[...], precision=HIGHEST)
    decay = jnp.exp(-t * dl_ref[...])
    hf = h[:, :HYENA_WIDTH] * decay
    hb = jnp.where(pos == 0.0, 0.0, h[:, HYENA_WIDTH:] * decay)
    h_ref[:, :HYENA_WIDTH] = hf
    h_ref[:, HYENA_WIDTH:] = hb

    @pl.when(i == 0)
    def _():
        n_ref[...] = jnp.zeros_like(n_ref)
    colsum = jnp.concatenate([jnp.sum(jnp.abs(hf), axis=0, keepdims=True),
                              jnp.sum(jnp.abs(hb), axis=0, keepdims=True)], axis=1)
    n_ref[...] += jnp.broadcast_to(colsum, n_ref.shape)


def _hyena_filter(length, fr, w1, b1, w2, b2, w3, f0, f1, deltas):
    tf = min(TM, length)
    consts = [fr, w1, b1, w2, b2, w3, f0, f1, deltas]
    return pl.pallas_call(
        functools.partial(_filter_kernel, length=length, tf=tf),
        grid=(length // tf,),
        in_specs=[_full_spec(c.shape) for c in consts],
        out_specs=[pl.BlockSpec((tf, 2 * HYENA_WIDTH), lambda i: (i, 0)),
                   pl.BlockSpec((8, 2 * HYENA_WIDTH), lambda i: (0, 0))],
        out_shape=[jax.ShapeDtypeStruct((length, 2 * HYENA_WIDTH), F32),
                   jax.ShapeDtypeStruct((8, 2 * HYENA_WIDTH), F32)],
        compiler_params=_params(("arbitrary",)), name="hyena_filter",
    )(*consts)


DFT_KG = 4
DFT_J = 8
DFT_WL = 512


def _dft_fwd_kernel(f_ref, x_ref, ar_ref, ai_ref):
    n1h, jj, wl = x_ref.shape
    n1 = ar_ref.shape[0]
    xs = x_ref[...].reshape(n1h * jj, wl).astype(BF16)
    ck = min(1024, n1 * jj)
    per = ck // jj
    for c in range(2 * n1 * jj // ck):
        r = _dot(f_ref[c * ck:(c + 1) * ck, :], xs).reshape(per, jj, wl)
        k0 = c * per
        if k0 < n1:
            ar_ref[k0:k0 + per] = r
        else:
            ai_ref[k0 - n1:k0 - n1 + per] = r


def _dft_fwd(f1k, x4):
    g, _, n2, w = x4.shape
    n1 = f1k.shape[0] // (2 * DFT_J)
    wl = min(DFT_WL, w)
    out_spec = pl.BlockSpec((None, n1, DFT_J, wl), lambda b, j, l: (b, 0, j, l))
    return pl.pallas_call(
        _dft_fwd_kernel,
        grid=(g, n2 // DFT_J, w // wl),
        in_specs=[pl.BlockSpec(f1k.shape, lambda b, j, l: (0, 0)),
                  pl.BlockSpec((None, n1 // 2, DFT_J, wl), lambda b, j, l: (b, 0, j, l))],
        out_specs=[out_spec, out_spec],
        out_shape=[jax.ShapeDtypeStruct((g, n1, n2, w), F32)] * 2,
        compiler_params=_params(("parallel", "parallel", "parallel")), name="hyena_dft_outer",
    )(f1k, x4)


def _spec_filter_kernel(m_ref, ar_ref, ai_ref, n_ref, gr_ref, gi_ref, *, inv_n):
    w = HYENA_WIDTH
    nrm = n_ref[0:1, :w] + n_ref[0:1, w:]
    inv = inv_n / nrm
    for kk in range(m_ref.shape[0]):
        x = jnp.concatenate([ar_ref[kk], ai_ref[kk]], axis=0).astype(BF16)
        y = _dot(m_ref[kk], x)
        n2 = y.shape[0] // 2
        gr_ref[kk] = (y[:n2, :w] + y[:n2, w:]) * inv
        gi_ref[kk] = (y[n2:, :w] - y[n2:, w:]) * inv


def _spec_kernel(m_ref, ar_ref, ai_ref, gr_ref, gi_ref, cr_ref, ci_ref):
    for kk in range(m_ref.shape[0]):
        mk = m_ref[kk]
        gr, gi = gr_ref[kk], gi_ref[kk]
        for b in range(ar_ref.shape[0]):
            x = jnp.concatenate([ar_ref[b, kk], ai_ref[b, kk]], axis=0).astype(BF16)
            y = _dot(mk, x)
            n2 = y.shape[0] // 2
            yr, yi = y[:n2], y[n2:]
            p = jnp.concatenate([yr * gr - yi * gi, yr * gi + yi * gr], axis=0).astype(BF16)
            c = _dot(mk, p, TN_DIMS)
            cr_ref[b, kk] = c[:n2]
            ci_ref[b, kk] = c[n2:]


def _spec_filter(mtab, ar, ai, nrm, n_total):
    n1 = mtab.shape[0]
    kg = min(DFT_KG, n1)
    w2 = ar.shape[-1]
    a_spec = pl.BlockSpec((kg, DFT_N2, w2), lambda j: (j, 0, 0))
    g_spec = pl.BlockSpec((kg, DFT_N2, HYENA_WIDTH), lambda j: (j, 0, 0))
    return pl.pallas_call(
        functools.partial(_spec_filter_kernel, inv_n=1.0 / n_total),
        grid=(n1 // kg,),
        in_specs=[pl.BlockSpec((kg, 2 * DFT_N2, 2 * DFT_N2), lambda j: (j, 0, 0)), a_spec, a_spec,
                  _full_spec(nrm.shape)],
        out_specs=[g_spec, g_spec],
        out_shape=[jax.ShapeDtypeStruct((n1, DFT_N2, HYENA_WIDTH), F32)] * 2,
        compiler_params=_params(("parallel",)), name="hyena_filter_spectrum",
    )(mtab, ar, ai, nrm)


def _spec(mtab, ar, ai, gr, gi):
    nb, n1 = ar.shape[0], ar.shape[1]
    kg = min(DFT_KG, n1)
    w = HYENA_WIDTH
    a_spec = pl.BlockSpec((nb, kg, DFT_N2, w), lambda j: (0, j, 0, 0))
    g_spec = pl.BlockSpec((kg, DFT_N2, w), lambda j: (j, 0, 0))
    return pl.pallas_call(
        _spec_kernel,
        grid=(n1 // kg,),
        in_specs=[pl.BlockSpec((kg, 2 * DFT_N2, 2 * DFT_N2), lambda j: (j, 0, 0)), a_spec, a_spec, g_spec, g_spec],
        out_specs=[a_spec, a_spec],
        out_shape=[jax.ShapeDtypeStruct(ar.shape, F32)] * 2,
        compiler_params=_params(("parallel",)), name="hyena_spectrum",
    )(mtab, ar, ai, gr, gi)


def _dft_inv_kernel(f_ref, cr_ref, ci_ref, x0_ref, zs_ref, o_ref):
    n1, jj, wl = cr_ref.shape
    c = jnp.concatenate([cr_ref[...].reshape(n1 * jj, wl), ci_ref[...].reshape(n1 * jj, wl)], axis=0).astype(BF16)
    y = _dot(f_ref[...], c).reshape(x0_ref.shape)
    o_ref[...] = x0_ref[...] * y + zs_ref[...]


def _dft_inv(finvk, cr, ci, x0v, zsv):
    nb, n1, n2, w = cr.shape
    c_spec = pl.BlockSpec((None, n1, DFT_J, w), lambda b, j: (b, 0, j, 0))
    t_spec = pl.BlockSpec((None, n1 // 2, DFT_J, w), lambda b, j: (b, 0, j, 0))
    return pl.pallas_call(
        _dft_inv_kernel,
        grid=(nb, n2 // DFT_J),
        in_specs=[pl.BlockSpec(finvk.shape, lambda b, j: (0, 0)), c_spec, c_spec, t_spec, t_spec],
        out_specs=t_spec,
        out_shape=jax.ShapeDtypeStruct(x0v.shape, F32),
        compiler_params=_params(("parallel", "parallel")), name="hyena_idft_outer",
    )(finvk, cr, ci, x0v, zsv)


def _hy_ctx_kernel(fc_ref, fi_ref, z_ref, h_ref, x0_ref, zs_ref, prev_ref, o_ref, *, n_total):
    del prev_ref
    w = HYENA_WIDTH
    h = h_ref[...]
    nk = fc_ref.shape[0] // 2
    nrm = jnp.sum(jnp.abs(h[:, :w]), axis=0, keepdims=True) + jnp.sum(jnp.abs(h[:, w:]), axis=0, keepdims=True)
    inv = (1.0 / n_total) / nrm
    hs = _dot(fc_ref[...], h.astype(BF16))
    gr = (hs[:nk, :w] + hs[:nk, w:]) * inv
    gi = (hs[nk:, :w] - hs[nk:, w:]) * inv
    zsp = _dot(fc_ref[...], z_ref[...].astype(BF16))
    zr, zi = zsp[:nk], zsp[nk:]
    p = jnp.concatenate([zr * gr - zi * gi, zr * gi + zi * gr], axis=0).astype(BF16)
    y = _dot(fi_ref[...], p)
    o_ref[...] = x0_ref[...] * y + zs_ref[...]


def _hy_ctx(fc, fi, z, hcat, x0, zs, out, nb, seq, ctx):
    n = seq + ctx
    w = HYENA_WIDTH
    row_spec = pl.BlockSpec((ctx, w), lambda b: (b * (n // ctx) + seq // ctx, 0))
    return pl.pallas_call(
        functools.partial(_hy_ctx_kernel, n_total=2 * ctx),
        grid=(nb,),
        in_specs=[_full_spec(fc.shape), _full_spec(fi.shape), row_spec, _full_spec(hcat.shape), row_spec, row_spec,
                  pl.BlockSpec(memory_space=pl.ANY)],
        out_specs=row_spec,
        out_shape=jax.ShapeDtypeStruct(out.shape, F32),
        input_output_aliases={6: 0},
        compiler_params=_params(("parallel",)), name="hyena_context",
    )(fc, fi, z, hcat, x0, zs, out)


def _dft_tables(seq, ctx):
    n = 2 * seq
    n1 = n // DFT_N2
    two_pi = 2.0 * math.pi

    def cs(num, den):
        ang = two_pi * (num % den).astype(F32) / den
        return jnp.cos(ang), jnp.sin(ang)
    k1 = jnp.arange(n1, dtype=jnp.int32)
    c, s = cs(k1[:, None] * k1[None, :n1 // 2], n1)
    eye = jnp.eye(DFT_J, dtype=F32)
    f1k = jnp.kron(jnp.concatenate([c, -s], axis=0), eye).astype(BF16)
    finvk = jnp.kron(jnp.concatenate([c.T, -s.T], axis=1), eye).astype(BF16)
    k2 = jnp.arange(DFT_N2, dtype=jnp.int32)
    phase = k2[None, None, :] * (k1[:, None, None] + n1 * k2[None, :, None])
    c, s = cs(phase, n)
    mtab = jnp.concatenate([jnp.concatenate([c, s], axis=2), jnp.concatenate([-s, c], axis=2)], axis=1)
    nc = 2 * ctx
    kk = jnp.arange(nc, dtype=jnp.int32)
    c, s = cs(kk[:, None] * kk[None, :ctx], nc)
    fc = jnp.concatenate([c, -s], axis=0).astype(BF16)
    fi = jnp.concatenate([c.T, -s.T], axis=1).astype(BF16)
    return f1k, finvk, mtab.astype(BF16), fc, fi


N_CLASSES = N_GROUPS * 6
CLS_ROWS = 32
PAIR_LO = (0, 0, 0, 1, 1, 2)
PAIR_HI = (1, 2, 3, 2, 3, 3)
HEXT = LANES


def _post_kernel(a1_ref, a2_ref, w1_ref, w2_ref, x_ref, mod_ref, g_ref, rw_ref, rb_ref,
                 xo_ref, hx_ref, meta_ref, cnt_ref, carry_ref):
    step = pl.program_id(0)

    @pl.when(step == 0)
    def _():
        carry_ref[...] = jnp.zeros_like(carry_ref)
    ml = _dot(a1_ref[...].astype(BF16), w1_ref[...]) + _dot(a2_ref[...], w2_ref[...])
    x = x_ref[...] + mod_ref[2:3, :] * ml
    xo_ref[...] = x
    hf = _modulate(x, g_ref[...], mod_ref[3:4, :], mod_ref[4:5, :])
    d = hf.shape[1]
    hx_ref[:, :d] = hf
    h = hf.astype(BF16)
    scores = jax.nn.sigmoid(_dot(rw_ref[...], h, NT_DIMS))
    biased = scores + rb_ref[...]
    sc = [scores[e:e + 1, :] for e in range(N_EXPERTS)]
    bi = [biased[e:e + 1, :] for e in range(N_EXPERTS)]
    best = None
    sel = None
    for g in range(N_GROUPS):
        v0, v1, v2, v3 = bi[4 * g:4 * g + 4]
        top1 = jnp.maximum(jnp.maximum(v0, v1), jnp.maximum(v2, v3))
        top2 = jnp.maximum(jnp.maximum(jnp.minimum(v0, v1), jnp.minimum(v2, v3)),
                           jnp.minimum(jnp.maximum(v0, v1), jnp.maximum(v2, v3)))
        gs = top1 + top2
        if g == 0:
            best, sel = gs, jnp.zeros_like(gs, dtype=jnp.int32)
        else:
            upd = gs > best
            best = jnp.where(upd, gs, best)
            sel = jnp.where(upd, g, sel)

    def pick(vals, j):
        out = vals[j]
        for g in range(1, N_GROUPS):
            out = jnp.where(sel == g, vals[4 * g + j], out)
        return out
    b = [pick(bi, j) for j in range(EXPERTS_PER_GROUP)]
    s = [pick(sc, j) for j in range(EXPERTS_PER_GROUP)]
    i1 = jnp.zeros_like(sel)
    m1 = b[0]
    for j in range(1, EXPERTS_PER_GROUP):
        upd = b[j] > m1
        m1 = jnp.where(upd, b[j], m1)
        i1 = jnp.where(upd, j, i1)
    i2 = jnp.full_like(sel, -1)
    m2 = jnp.full_like(m1, -jnp.inf)
    for j in range(EXPERTS_PER_GROUP):
        upd = jnp.logical_and(i1 != j, b[j] > m2)
        m2 = jnp.where(upd, b[j], m2)
        i2 = jnp.where(upd, j, i2)
    w1 = s[0]
    w2 = s[0]
    for j in range(1, EXPERTS_PER_GROUP):
        w1 = jnp.where(i1 == j, s[j], w1)
        w2 = jnp.where(i2 == j, s[j], w2)
    tot = w1 + w2
    first_lo = i1 < i2
    lo = jnp.minimum(i1, i2)
    hi = jnp.maximum(i1, i2)
    cls = sel * 6 + jnp.where(lo == 0, 0, jnp.where(lo == 1, 3, 5)) + hi - lo - 1
    w_lo = jnp.where(first_lo, w1, w2) / tot
    w_hi = jnp.where(first_lo, w2, w1) / tot
    lrow = lax.broadcasted_iota(jnp.int32, (LANES, 1), 0)
    wt = jnp.where(lrow == 0, w_lo, jnp.where(lrow == 1, w_hi, 0.0))
    hx_ref[:, d:] = wt.T
    tm = cls.shape[1]
    crow = lax.broadcasted_iota(jnp.int32, (CLS_ROWS, 1), 0)
    onehot = crow == cls
    tri = (lax.broadcasted_iota(jnp.int32, (tm, tm), 0) <= lax.broadcasted_iota(jnp.int32, (tm, tm), 1))
    cum = _dot(onehot.astype(BF16), tri.astype(BF16))
    carry = carry_ref[...]
    rank = jnp.sum(jnp.where(onehot, carry[:, 0:1] + cum, 0.0), axis=0, keepdims=True) - 1.0
    mrow = lax.broadcasted_iota(jnp.int32, (8, 1), 0)
    meta_ref[...] = jnp.where(mrow == 0, cls, jnp.where(mrow == 1, rank.astype(jnp.int32), 0))
    carry = carry + cum[:, tm - 1:tm]
    carry_ref[...] = carry
    cnt_ref[...] = carry


def _post(a1, a2, w1, w2, x, mod, g, rw, rb, ntb):
    rows, d = x.shape
    row = _mod_row(ntb)
    mod_spec = pl.BlockSpec((None, 8, d), lambda i: (row(i), 0, 0))
    nt = rows // TM
    return pl.pallas_call(
        _post_kernel,
        grid=(nt,),
        in_specs=[_tok_spec(a1.shape[1]), _tok_spec(a2.shape[1]), _full_spec(w1.shape), _full_spec(w2.shape),
                  _tok_spec(d), mod_spec, _full_spec(g.shape), _full_spec(rw.shape), _full_spec(rb.shape)],
        out_specs=[_tok_spec(d), _tok_spec(d + HEXT), pl.BlockSpec((None, 8, TM), lambda i: (i, 0, 0)),
                   pl.BlockSpec((CLS_ROWS, LANES), lambda i: (0, 0))],
        out_shape=[jax.ShapeDtypeStruct((rows, d), F32), jax.ShapeDtypeStruct((rows, d + HEXT), F32),
                   jax.ShapeDtypeStruct((nt, 8, TM), jnp.int32), jax.ShapeDtypeStruct((CLS_ROWS, LANES), F32)],
        scratch_shapes=[pltpu.VMEM((CLS_ROWS, LANES), F32)],
        compiler_params=_params(("arbitrary",)), name="post_mixer",
    )(a1, a2, w1, w2, x, mod, g, rw, rb)


MOE_TM = 256
MOE_TD = 512
MOE_UNROLL = 8


def _route_plan(meta, cnt, rows):
    counts = cnt[:N_CLASSES, 0].astype(jnp.int32)
    padded = ((counts + MOE_TM - 1) // MOE_TM) * MOE_TM
    ends = jnp.cumsum(padded)
    offs = ends - padded
    cls = meta[:, 0, :].reshape(rows)
    rank = meta[:, 1, :].reshape(rows)
    slot = offs[cls] + rank
    ntiles = rows // MOE_TM + N_CLASSES
    starts = jnp.arange(ntiles, dtype=jnp.int32) * MOE_TM
    tcls = jnp.sum((ends[None, :] <= starts[:, None]).astype(jnp.int32), axis=1)
    valid = tcls < N_CLASSES
    nvalid = jnp.sum(valid.astype(jnp.int32))
    last = tcls[jnp.maximum(nvalid - 1, 0)]
    tcls = jnp.where(valid, tcls, jnp.minimum(last, N_CLASSES - 1))
    grp, pair = tcls // 6, tcls % 6
    ea = grp * EXPERTS_PER_GROUP + jnp.asarray(PAIR_LO, jnp.int32)[pair]
    eb = grp * EXPERTS_PER_GROUP + jnp.asarray(PAIR_HI, jnp.int32)[pair]
    return slot, ea, eb, valid.astype(jnp.int32), ntiles


def _row_copy(src, dst, sem, s_row, d_row):
    return pltpu.make_async_copy(src.at[pl.ds(s_row, 1), :], dst.at[pl.ds(d_row, 1), :], sem)


def _dispatch_kernel(slot_ref, hx_ref, init_ref, xs_ref, sem):
    del init_ref
    td = hx_ref.shape[0]

    def issue(t, c):
        _row_copy(hx_ref, xs_ref, sem, t, slot_ref[0, t]).start()
        return c
    lax.fori_loop(0, td, issue, 0, unroll=MOE_UNROLL)
    pltpu.make_async_copy(hx_ref, xs_ref.at[pl.ds(0, td), :], sem).wait()


def _dispatch(hx, slot, nrows_sorted):
    rows, width = hx.shape
    nsteps = rows // MOE_TD
    init = jnp.zeros((nrows_sorted, width), F32)
    return pl.pallas_call(
        _dispatch_kernel,
        grid=(nsteps,),
        in_specs=[pl.BlockSpec((None, 1, MOE_TD), lambda i: (i, 0, 0), memory_space=pltpu.SMEM),
                  pl.BlockSpec((MOE_TD, width), lambda i: (i, 0)),
                  pl.BlockSpec(memory_space=pl.ANY)],
        out_specs=pl.BlockSpec(memory_space=pl.ANY),
        out_shape=jax.ShapeDtypeStruct((nrows_sorted, width), F32),
        scratch_shapes=[pltpu.SemaphoreType.DMA(())],
        input_output_aliases={2: 0},
        compiler_params=_params(("arbitrary",)), name="moe_dispatch",
    )(slot.reshape(nsteps, 1, MOE_TD), hx, init)


def _experts_kernel(ea_ref, eb_ref, valid_ref, xs_ref, wga_ref, wua_ref, wda_ref, wgb_ref, wub_ref, wdb_ref, ys_ref):
    del ea_ref, eb_ref
    j = pl.program_id(0)

    @pl.when(valid_ref[j] == 0)
    def _():
        ys_ref[...] = jnp.zeros_like(ys_ref)

    @pl.when(valid_ref[j] != 0)
    def _():
        d = ys_ref.shape[1]
        x = xs_ref[:, :d].astype(BF16)

        def ffn(wg_ref, wu_ref, wd_ref, w):
            gate = _dot(x, wg_ref[...])
            up = _dot(x, wu_ref[...])
            a = (gate * jax.nn.sigmoid(gate) * up * w).astype(BF16)
            return _dot(a, wd_ref[...])
        ys_ref[...] = (ffn(wga_ref, wua_ref, wda_ref, xs_ref[:, d:d + 1])
                       + ffn(wgb_ref, wub_ref, wdb_ref, xs_ref[:, d + 1:d + 2]))


def _experts(xs, ea, eb, valid, wg, wu, wd, layer, ntiles):
    d = wg.shape[2]
    ff = wg.shape[3]
    gu_a = pl.BlockSpec((None, None, d, ff), lambda j, ea, eb, v: (layer, ea[j], 0, 0))
    gu_b = pl.BlockSpec((None, None, d, ff), lambda j, ea, eb, v: (layer, eb[j], 0, 0))
    dn_a = pl.BlockSpec((None, None, ff, d), lambda j, ea, eb, v: (layer, ea[j], 0, 0))
    dn_b = pl.BlockSpec((None, None, ff, d), lambda j, ea, eb, v: (layer, eb[j], 0, 0))
    return pl.pallas_call(
        _experts_kernel,
        grid_spec=pltpu.PrefetchScalarGridSpec(
            num_scalar_prefetch=3, grid=(ntiles,),
            in_specs=[pl.BlockSpec((MOE_TM, xs.shape[1]), lambda j, ea, eb, v: (j, 0)),
                      gu_a, gu_a, dn_a, gu_b, gu_b, dn_b],
            out_specs=pl.BlockSpec((MOE_TM, d), lambda j, ea, eb, v: (j, 0))),
        out_shape=jax.ShapeDtypeStruct((ntiles * MOE_TM, d), F32),
        compiler_params=_params(("arbitrary",)), name="moe_experts",
    )(ea, eb, valid, xs, wg, wu, wd, wg, wu, wd)


def _undispatch_kernel(slot_ref, ys_ref, y_ref, sem):
    td = y_ref.shape[0]

    def issue(t, c):
        _row_copy(ys_ref, y_ref, sem, slot_ref[0, t], t).start()
        return c
    lax.fori_loop(0, td, issue, 0, unroll=MOE_UNROLL)
    pltpu.make_async_copy(ys_ref.at[pl.ds(0, td), :], y_ref, sem).wait()


def _undispatch(ys, slot, rows):
    d = ys.shape[1]
    nsteps = rows // MOE_TD
    return pl.pallas_call(
        _undispatch_kernel,
        grid=(nsteps,),
        in_specs=[pl.BlockSpec((None, 1, MOE_TD), lambda i: (i, 0, 0), memory_space=pltpu.SMEM),
                  pl.BlockSpec(memory_space=pl.ANY)],
        out_specs=pl.BlockSpec((MOE_TD, d), lambda i: (i, 0)),
        out_shape=jax.ShapeDtypeStruct((rows, d), F32),
        scratch_shapes=[pltpu.SemaphoreType.DMA(())],
        compiler_params=_params(("arbitrary",)), name="moe_undispatch",
    )(slot.reshape(nsteps, 1, MOE_TD), ys)


def _moe(hx, meta, cnt, wg, wu, wd, layer):
    rows = hx.shape[0]
    slot, ea, eb, valid, ntiles = _route_plan(meta, cnt, rows)
    xs = _dispatch(hx, slot, ntiles * MOE_TM)
    ys = _experts(xs, ea, eb, valid, wg, wu, wd, layer, ntiles)
    return _undispatch(ys, slot, rows)


def _final_kernel(x_ref, y_ref, mod_ref, o_ref):
    o_ref[...] = x_ref[...] + mod_ref[5:6, :] * y_ref[...]


def _final(x, y, mod, nb, seq, ctx):
    d = x.shape[1]
    ntb = (seq + ctx) // TM
    nlt = seq // TM
    tok = pl.BlockSpec((TM, d), lambda b, j: (b * ntb + j, 0))
    return pl.pallas_call(
        _final_kernel,
        grid=(nb, nlt),
        in_specs=[tok, tok, pl.BlockSpec((None, 8, d), lambda b, j: (b, 0, 0))],
        out_specs=pl.BlockSpec((None, TM, d), lambda b, j: (b, j, 0)),
        out_shape=jax.ShapeDtypeStruct((nb, seq, d), F32),
        compiler_params=_params(("parallel", "parallel")), name="final_residual",
    )(x, y, mod)


def _pad_heads(w, heads, dim, axis):
    shp = w.shape
    w = w.reshape(shp[:axis] + (heads, dim) + shp[axis + 1:])
    pad = [(0, 0)] * w.ndim
    pad[axis + 1] = (0, LANES - dim)
    w = jnp.pad(w, pad)
    return w.reshape(shp[:axis] + (heads * LANES,) + shp[axis + 1:])


def _pad_vec(v, mult=1.0):
    return jnp.pad(v.astype(F32) * mult, (0, LANES - v.shape[0])).reshape(1, LANES)


def _rope_tables(seq, ctx, d_rot, off):
    rows = seq // GRID_W
    row = jnp.repeat(jnp.arange(rows), GRID_W).astype(F32)
    col = jnp.tile(jnp.arange(GRID_W), rows).astype(F32)
    n_freq = d_rot // 4
    inv = ROPE_BASE ** (-jnp.arange(n_freq, dtype=F32) / n_freq)
    ang = jnp.concatenate([row[:, None] * inv, col[:, None] * inv], axis=-1)
    cos, sin = jnp.cos(ang), jnp.sin(ang)
    half = d_rot // 2

    def z(r, w):
        return jnp.zeros((r, w), F32)
    rest = LANES - off - 2 * half
    cos_l = jnp.concatenate([jnp.ones((seq, off), F32), cos, cos, z(seq, rest)], axis=1)
    sa_l = jnp.concatenate([z(seq, off + half), sin, z(seq, rest)], axis=1)
    sb_l = jnp.concatenate([z(seq, off), -sin, z(seq, half + rest)], axis=1)
    cos_c = jnp.concatenate([jnp.ones((ctx, off + 2 * half), F32), z(ctx, rest)], axis=1)
    return (jnp.concatenate([cos_l, cos_c], axis=0), jnp.concatenate([sa_l, z(ctx, LANES)], axis=0),
            jnp.concatenate([sb_l, z(ctx, LANES)], axis=0))


def _forward(x, c, ctx, c_ctx, ada_w, ada_b, norm_g, ev_w_in, ev_w_out, pool_w, pool_scale,
             swa_q_gain, swa_k_gain, swa_sink, od_w_in, od_w_out, hy_conv_w, hy_conv_b,
             hy_w1, hy_b1, hy_w2, hy_b2, hy_w3, hy_freq, hy_skip, mla_cq_gain, mla_ckv_gain,
             mla_w_uq, mla_w_ukv, mla_q_gain, mla_k_gain, router_w, router_b,
             moe_w_gate, moe_w_up, moe_w_down):
    nb, seq, d = x.shape
    nctx = ctx.shape[1]
    depth = ada_w.shape[0]
    assert nctx == TM and seq % (2 * TM) == 0 and seq % GRID_W == 0 and nb <= 2
    n = seq + nctx
    ntb = n // TM
    rows = nb * n

    cvec = jnp.concatenate([c, c_ctx[None, :], jnp.zeros((8 - nb - 1, d), F32)], axis=0)
    if nb == 1:
        cvec = jnp.concatenate([c, jnp.zeros((1, d), F32), c_ctx[None, :], jnp.zeros((5, d), F32)], axis=0)
    mod = _adaln(cvec, ada_w, ada_b)

    xs = jnp.concatenate([x, ctx], axis=1).reshape(rows, d)
    tabs_swa = _rope_tables(seq, nctx, HEAD_DIM, 0)
    tabs_mla = _rope_tables(seq, nctx, MLA_ROPE, MLA_NOPE)
    f1k, finvk, mtab, fc, fi = _dft_tables(seq, nctx)
    bands = (HYENA_EMB - 1) // 2
    frv = jnp.linspace(1e-4, bands - 1, bands, dtype=F32)
    fr = jnp.concatenate([jnp.zeros((1,), F32), frv, frv, jnp.zeros((LANES - 1 - 2 * bands,), F32)]).reshape(1, LANES)
    deltas = jnp.abs(jnp.linspace(math.log(HYENA_TARGET) / HYENA_FAST_DECAY,
                                  math.log(HYENA_TARGET) / HYENA_SLOW_DECAY, HYENA_WIDTH, dtype=F32)).reshape(1, -1)
    rw = jnp.transpose(router_w).astype(BF16)
    rb = router_b.astype(F32).reshape(N_EXPERTS, 1)

    wg16, wu16, wd16 = moe_w_gate.astype(BF16), moe_w_up.astype(BF16), moe_w_down.astype(BF16)

    prev = None
    y = None
    for layer in range(depth):
        i = layer // 2
        lmod = mod[layer]
        g1 = norm_g[layer, 0].reshape(1, d)
        g2 = norm_g[layer, 1].reshape(1, d)
        if layer % 2 == 0:
            w = ev_w_in[i]
            o1 = POOL_WIDTH
            o2 = o1 + SWA_Q_HEADS * HEAD_DIM
            o3 = o2 + SWA_KV_HEADS * HEAD_DIM
            w_in = jnp.concatenate([w[:, :o1], _pad_heads(w[:, o1:o2], SWA_Q_HEADS, HEAD_DIM, 1),
                                    _pad_heads(w[:, o2:o3], SWA_KV_HEADS, HEAD_DIM, 1),
                                    _pad_heads(w[:, o3:], SWA_KV_HEADS, HEAD_DIM, 1)], axis=1).astype(BF16)
            qg = _pad_vec(swa_q_gain[i], HEAD_DIM ** -0.5)
            kg = _pad_vec(swa_k_gain[i])
            xs, a, q, k, v = _even_in(xs, prev, lmod, g1, w_in, qg, kg, tabs_swa, ntb)
            w_bd = jnp.zeros((POOL_WIDTH, POOL_WIDTH), F32)
            for g in range(POOL_GROUPS):
                sl = slice(g * POOL_GROUP_DIM, (g + 1) * POOL_GROUP_DIM)
                w_bd = w_bd.at[sl, sl].set(pool_w[i, g])
            mix1 = _pool(a, w_bd.astype(BF16), pool_scale[i].reshape(1, -1), ntb, seq, nctx)
            mix2 = _swa(q, k, v, swa_sink[i].astype(F32), nb, seq, nctx)
            wo = ev_w_out[i]
            wo1 = wo[:POOL_WIDTH].astype(BF16)
            wo2 = _pad_heads(wo[POOL_WIDTH:], SWA_Q_HEADS, HEAD_DIM, 0).astype(BF16)
        else:
            w_in = jnp.pad(od_w_in[i], ((0, 0), (0, LANES - MLA_ROPE))).astype(BF16)
            wuq = _pad_heads(mla_w_uq[i], MLA_HEADS, MLA_QK, 1).astype(BF16)
            wukv = mla_w_ukv[i].reshape(MLA_KV_RANK, MLA_HEADS, MLA_NOPE + MLA_V)
            wuk = _pad_heads(wukv[:, :, :MLA_NOPE].reshape(MLA_KV_RANK, -1), MLA_HEADS, MLA_NOPE, 1).astype(BF16)
            wuv = _pad_heads(wukv[:, :, MLA_NOPE:].reshape(MLA_KV_RANK, -1), MLA_HEADS, MLA_V, 1).astype(BF16)
            qg = _pad_vec(mla_q_gain[i], MLA_QK ** -0.5 * LOG2E)
            kg = _pad_vec(mla_k_gain[i])
            xs, u, q, k, v = _odd_in(xs, prev, lmod, g1, w_in, mla_cq_gain[i].reshape(1, -1),
                                     mla_ckv_gain[i].reshape(1, -1), wuq, wuk, wuv, qg, kg, tabs_mla, ntb)
            x0, zs, z = _hy_pre(u, hy_conv_w[i], hy_conv_b[i].reshape(1, -1), hy_skip[i].reshape(1, -1), ntb)
            w1p = jnp.zeros((LANES, LANES), F32).at[:HYENA_EMB, :HYENA_HIDDEN].set(hy_w1[i])
            w2p = jnp.zeros((LANES, LANES), F32).at[:HYENA_HIDDEN, :HYENA_HIDDEN].set(hy_w2[i])
            w3p = jnp.zeros((LANES, 2 * HYENA_WIDTH), F32).at[:HYENA_HIDDEN].set(hy_w3[i])
            fparams = (fr, w1p, _pad_vec(hy_b1[i]), w2p, _pad_vec(hy_b2[i]), w3p,
                       _pad_vec(hy_freq[i, 0]), _pad_vec(hy_freq[i, 1]), deltas)
            hcat, nrm = _hyena_filter(seq, *fparams)
            hcat_c, _ = _hyena_filter(nctx, *fparams)
            n1 = 2 * seq // DFT_N2
            w = HYENA_WIDTH
            har, hai = _dft_fwd(f1k, hcat.reshape(1, n1 // 2, DFT_N2, 2 * w))
            gr, gi = _spec_filter(mtab, har[0], hai[0], nrm, 2 * seq)
            view = (nb, n // DFT_N2, DFT_N2, w)
            ar, ai = _dft_fwd(f1k, z.reshape(view))
            cr, ci = _spec(mtab, ar, ai, gr, gi)
            hy = _dft_inv(finvk, cr, ci, x0.reshape(view), zs.reshape(view))
            mix1 = _hy_ctx(fc, fi, z, hcat_c, x0, zs, hy.reshape(rows, w), nb, seq, nctx)
            mix2 = _mla(q, k, v, nb, seq, nctx)
            wo = od_w_out[i]
            wo1 = wo[:HYENA_WIDTH].astype(BF16)
            wo2 = _pad_heads(wo[HYENA_WIDTH:], MLA_HEADS, MLA_V, 0).astype(BF16)
        xs, hx, meta, cnt = _post(mix1, mix2, wo1, wo2, xs, lmod, g2, rw, rb, ntb)
        y = _moe(hx, meta, cnt, wg16, wu16, wd16, layer)
        prev = (y, lmod)
    return _final(xs, y, mod[depth - 1], nb, seq, nctx)


def kernel(x, c, ctx, c_ctx, ada_w, ada_b, norm_g, ev_w_in, ev_w_out, pool_w, pool_scale, swa_q_gain, swa_k_gain, swa_sink, od_w_in, od_w_out, hy_conv_w, hy_conv_b, hy_w1, hy_b1, hy_w2, hy_b2, hy_w3, hy_freq, hy_skip, mla_cq_gain, mla_ckv_gain, mla_w_uq, mla_w_ukv, mla_q_gain, mla_k_gain, router_w, router_b, moe_w_gate, moe_w_up, moe_w_down):
    return _forward(x, c, ctx, c_ctx, ada_w, ada_b, norm_g, ev_w_in, ev_w_out, pool_w, pool_scale,
                    swa_q_gain, swa_k_gain, swa_sink, od_w_in, od_w_out, hy_conv_w, hy_conv_b,
                    hy_w1, hy_b1, hy_w2, hy_b2, hy_w3, hy_freq, hy_skip, mla_cq_gain, mla_ckv_gain,
                    mla_w_uq, mla_w_ukv, mla_q_gain, mla_k_gain, router_w, router_b,
                    moe_w_gate, moe_w_up, moe_w_down)
```

```python
import functools
import math

import jax
import jax.numpy as jnp
from jax import lax
from jax.experimental import pallas as pl
from jax.experimental.pallas import tpu as pltpu

F32 = jnp.float32
BF16 = jnp.bfloat16

GRID_W = 64
HEAD_DIM = 64
ROPE_BASE = 10000.0
EPS = 1e-6
POOL_GROUPS = 4
POOL_GROUP_DIM = 64
POOL_WIDTH = POOL_GROUPS * POOL_GROUP_DIM
POOL_WINDOWS = (2, 4, 8, 16)
SWA_Q_HEADS = 12
SWA_KV_HEADS = 4
SWA_GROUP = SWA_Q_HEADS // SWA_KV_HEADS
SWA_WINDOW = 128
SWA_BLOCK = 128
HYENA_WIDTH = 512
HYENA_EMB = 33
HYENA_HIDDEN = 64
HYENA_FAST_DECAY = 0.3
HYENA_SLOW_DECAY = 1.5
HYENA_TARGET = 1e-2
MLA_HEADS = 8
MLA_NOPE = 64
MLA_ROPE = 32
MLA_QK = MLA_NOPE + MLA_ROPE
MLA_V = 64
MLA_Q_RANK = 256
MLA_KV_RANK = 128
N_EXPERTS = 16
N_GROUPS = 4
EXPERTS_PER_GROUP = N_EXPERTS // N_GROUPS
EXPERT_FF = 512

LANES = 128
TM = 256
DFT_N2 = 128
VMEM_LIMIT = 56 * 1024 * 1024

NT_DIMS = (((1,), (1,)), ((), ()))
TN_DIMS = (((0,), (0,)), ((), ()))


def _dot(a, b, dims=None, precision=None):
    if dims is None:
        return jnp.dot(a, b, preferred_element_type=F32, precision=precision)
    return lax.dot_general(a, b, dims, preferred_element_type=F32, precision=precision)


def _params(sem):
    return pltpu.CompilerParams(dimension_semantics=sem, vmem_limit_bytes=VMEM_LIMIT)


def _mod_row(ntb):
    def f(i):
        return jnp.where(i % ntb == ntb - 1, 2, i // ntb)
    return f


def _modulate(x, g, shift, scale):
    ms = jnp.mean(x * x, axis=-1, keepdims=True)
    return (x * lax.rsqrt(ms + EPS) * g) * (1.0 + scale) + shift


def _head_norm_rope(xh, real_dim, gain, cos, sa, sb, half):
    r = lax.rsqrt(jnp.sum(xh * xh, axis=-1, keepdims=True) * (1.0 / real_dim) + EPS)
    xn = xh * r * gain
    return xn * cos + pltpu.roll(xn, half, 1) * sa + pltpu.roll(xn, LANES - half, 1) * sb


def _adaln_kernel(c_ref, w_ref, b_ref, o_ref):
    c = c_ref[...]
    s = (c * jax.nn.sigmoid(c)).astype(BF16)
    o_ref[...] = _dot(s, w_ref[...].astype(BF16)) + b_ref[...]


def _adaln(cvec, ada_w, ada_b):
    depth, d, d6 = ada_w.shape
    nchunk = d6 // d
    out = pl.pallas_call(
        _adaln_kernel,
        grid=(depth, nchunk),
        in_specs=[pl.BlockSpec((8, d), lambda l, j: (0, 0)),
                  pl.BlockSpec((None, d, d), lambda l, j: (l, 0, j)),
                  pl.BlockSpec((None, 1, d), lambda l, j: (l, 0, j))],
        out_specs=pl.BlockSpec((None, None, 8, d), lambda l, j: (l, j, 0, 0)),
        out_shape=jax.ShapeDtypeStruct((depth, nchunk, 8, d), F32),
        compiler_params=_params(("parallel", "parallel")),
        name="adaln",
    )(cvec, ada_w, ada_b.reshape(depth, 1, d6))
    mod = jnp.transpose(out, (0, 2, 1, 3))
    return jnp.pad(mod, ((0, 0), (0, 0), (0, 8 - nchunk), (0, 0)))


def _even_in_kernel(*refs, has_prev):
    if has_prev:
        (x_ref, y_ref, pmod_ref, mod_ref, g_ref, w_ref, qg_ref, kg_ref, cos_ref, sa_ref, sb_ref,
         xo_ref, a_ref, q_ref, k_ref, v_ref) = refs
        x = x_ref[...] + pmod_ref[5:6, :] * y_ref[...]
        xo_ref[...] = x
    else:
        (x_ref, mod_ref, g_ref, w_ref, qg_ref, kg_ref, cos_ref, sa_ref, sb_ref,
         a_ref, q_ref, k_ref, v_ref) = refs
        x = x_ref[...]
    h = _modulate(x, g_ref[...], mod_ref[0:1, :], mod_ref[1:2, :]).astype(BF16)
    p = _dot(h, w_ref[...])
    a_ref[...] = p[:, :POOL_WIDTH]
    cos, sa, sb = cos_ref[...], sa_ref[...], sb_ref[...]
    o = POOL_WIDTH
    for hh in range(SWA_Q_HEADS):
        xh = p[:, o + LANES * hh:o + LANES * (hh + 1)]
        q_ref[:, LANES * hh:LANES * (hh + 1)] = _head_norm_rope(
            xh, HEAD_DIM, qg_ref[...], cos, sa, sb, HEAD_DIM // 2).astype(BF16)
    o += SWA_Q_HEADS * LANES
    for hh in range(SWA_KV_HEADS):
        xh = p[:, o + LANES * hh:o + LANES * (hh + 1)]
        k_ref[:, LANES * hh:LANES * (hh + 1)] = _head_norm_rope(
            xh, HEAD_DIM, kg_ref[...], cos, sa, sb, HEAD_DIM // 2).astype(BF16)
    o += SWA_KV_HEADS * LANES
    ones_hi = (lax.broadcasted_iota(jnp.int32, (1, LANES), 1) >= HEAD_DIM).astype(F32)
    for hh in range(SWA_KV_HEADS):
        vh = p[:, o + LANES * hh:o + LANES * (hh + 1)]
        v_ref[:, LANES * hh:LANES * (hh + 1)] = (vh + ones_hi).astype(BF16)


def _odd_in_kernel(x_ref, y_ref, pmod_ref, mod_ref, g_ref, w_ref, cqg_ref, ckvg_ref, wuq_ref, wuk_ref,
                   wuv_ref, cq_ref, sq_ref, ck_ref, sk_ref,
                   xo_ref, u_ref, q_ref, k_ref, v_ref):
    x = x_ref[...] + pmod_ref[5:6, :] * y_ref[...]
    xo_ref[...] = x
    h = _modulate(x, g_ref[...], mod_ref[0:1, :], mod_ref[1:2, :]).astype(BF16)
    p = _dot(h, w_ref[...])
    nu = 3 * HYENA_WIDTH
    u_ref[...] = p[:, :nu]
    cq = p[:, nu:nu + MLA_Q_RANK]
    ckv = p[:, nu + MLA_Q_RANK:nu + MLA_Q_RANK + MLA_KV_RANK]
    krb = p[:, nu + MLA_Q_RANK + MLA_KV_RANK:]
    cqn = (cq * lax.rsqrt(jnp.mean(cq * cq, axis=-1, keepdims=True) + EPS) * cqg_ref[...]).astype(BF16)
    ckvn = (ckv * lax.rsqrt(jnp.mean(ckv * ckv, axis=-1, keepdims=True) + EPS) * ckvg_ref[...]).astype(BF16)
    nh = MLA_HEADS * LANES
    qp = _dot(cqn, wuq_ref[...])
    kp = _dot(ckvn, wuk_ref[...])
    vp = _dot(ckvn, wuv_ref[...])
    half = MLA_ROPE // 2
    lane = lax.broadcasted_iota(jnp.int32, (1, LANES), 1)
    krp = pltpu.roll(krb, MLA_NOPE, 1)
    krot = (jnp.where(jnp.logical_and(lane >= MLA_NOPE + half, lane < MLA_NOPE + 2 * half),
                      pltpu.roll(krp, half, 1), 0.0)
            - jnp.where(jnp.logical_and(lane >= MLA_NOPE, lane < MLA_NOPE + half),
                        pltpu.roll(krp, LANES - half, 1), 0.0))
    cq_t, sq_t, ck_t, sk_t = cq_ref[...], sq_ref[...], ck_ref[...], sk_ref[...]
    ones_hi = (lane >= MLA_V).astype(F32)

    def norm(xh):
        return lax.rsqrt(jnp.sum(xh * xh, axis=-1, keepdims=True) * (1.0 / MLA_QK) + EPS)
    for hh in range(MLA_HEADS):
        sl = slice(LANES * hh, LANES * (hh + 1))
        xq = qp[:, sl]
        q_ref[:, sl] = ((xq * cq_t + qp[:, nh + LANES * hh:nh + LANES * (hh + 1)] * sq_t) * norm(xq)).astype(BF16)
        xk = kp[:, sl] + krp
        k_ref[:, sl] = ((xk * ck_t + krot * sk_t) * norm(xk)).astype(BF16)
        v_ref[:, sl] = (vp[:, sl] + ones_hi).astype(BF16)


def _tok_spec(width):
    return pl.BlockSpec((TM, width), lambda i: (i, 0))


def _full_spec(shape):
    nd = len(shape)
    return pl.BlockSpec(shape, lambda i: (0,) * nd)


def _even_in(x, prev, mod, g, w, qg, kg, tabs, ntb):
    rows, d = x.shape
    row = _mod_row(ntb)
    mod_spec = pl.BlockSpec((None, 8, d), lambda i: (row(i), 0, 0))
    tab_spec = pl.BlockSpec((TM, LANES), lambda i: (i % ntb, 0))
    nq, nk = SWA_Q_HEADS * LANES, SWA_KV_HEADS * LANES
    ins = [x]
    specs = [_tok_spec(d)]
    outs = []
    ospecs = []
    if prev is not None:
        y, pmod = prev
        ins += [y, pmod]
        specs += [_tok_spec(d), mod_spec]
        outs.append(jax.ShapeDtypeStruct((rows, d), F32))
        ospecs.append(_tok_spec(d))
    ins += [mod, g, w, qg, kg, *tabs]
    specs += [mod_spec, _full_spec(g.shape), _full_spec(w.shape), _full_spec(qg.shape), _full_spec(kg.shape),
              tab_spec, tab_spec, tab_spec]
    outs += [jax.ShapeDtypeStruct((rows, POOL_WIDTH), F32), jax.ShapeDtypeStruct((rows, nq), BF16),
             jax.ShapeDtypeStruct((rows, nk), BF16), jax.ShapeDtypeStruct((rows, nk), BF16)]
    ospecs += [_tok_spec(POOL_WIDTH), _tok_spec(nq), _tok_spec(nk), _tok_spec(nk)]
    res = pl.pallas_call(
        functools.partial(_even_in_kernel, has_prev=prev is not None),
        grid=(rows // TM,), in_specs=specs, out_specs=ospecs, out_shape=outs,
        compiler_params=_params(("parallel",)), name="even_in",
    )(*ins)
    if prev is None:
        return (x, *res)
    return res


def _odd_in(x, prev, mod, g, w, cqg, ckvg, wuq, wuk, wuv, tabs, ntb):
    rows, d = x.shape
    row = _mod_row(ntb)
    mod_spec = pl.BlockSpec((None, 8, d), lambda i: (row(i), 0, 0))
    tab_spec = pl.BlockSpec((TM, LANES), lambda i: (i % ntb, 0))
    y, pmod = prev
    nh = MLA_HEADS * LANES
    consts = [g, w, cqg, ckvg, wuq, wuk, wuv]
    return pl.pallas_call(
        _odd_in_kernel,
        grid=(rows // TM,),
        in_specs=[_tok_spec(d), _tok_spec(d), mod_spec, mod_spec] + [_full_spec(c.shape) for c in consts]
        + [tab_spec] * len(tabs),
        out_specs=[_tok_spec(d), _tok_spec(3 * HYENA_WIDTH), _tok_spec(nh), _tok_spec(nh), _tok_spec(nh)],
        out_shape=[jax.ShapeDtypeStruct((rows, d), F32), jax.ShapeDtypeStruct((rows, 3 * HYENA_WIDTH), F32),
                   jax.ShapeDtypeStruct((rows, nh), BF16), jax.ShapeDtypeStruct((rows, nh), BF16),
                   jax.ShapeDtypeStruct((rows, nh), BF16)],
        compiler_params=_params(("parallel",)), name="odd_in",
    )(x, y, pmod, mod, *consts, *tabs)


def _halo_specs(width, ntb):
    per = TM // 8

    def prev_map(i):
        return (jnp.maximum(i * per - 1, 0), 0)

    def next_map(i):
        return ((i + 1) * per - jnp.where(i % ntb == ntb - 1, 1, 0), 0)
    return pl.BlockSpec((8, width), prev_map), pl.BlockSpec((8, width), next_map)


def _halo_valid(i, ntb):
    j = i % ntb
    prev_ok = jnp.logical_and(j != 0, j != ntb - 1)
    next_ok = j < ntb - 2
    return prev_ok, next_ok


def _pool_kernel(a_ref, ap_ref, an_ref, w_ref, sc_ref, o_ref, *, ntb, seq, ctx):
    i = pl.program_id(0)
    prev_ok, next_ok = _halo_valid(i, ntb)
    a = a_ref[...]
    ap = jnp.where(prev_ok, ap_ref[...], 0.0)
    an = jnp.where(next_ok, an_ref[...], 0.0)
    ext = jnp.concatenate([ap, a, an], axis=0)
    rows_ext = TM + 16

    def shifted(d):
        return pltpu.roll(ext, (-d) % rows_ext, 0)[8:8 + TM]

    j = i % ntb
    is_ctx = j == ntb - 1
    pos = jnp.where(is_ctx, 0, j * TM) + lax.broadcasted_iota(jnp.int32, (TM, 1), 0)
    length = jnp.where(is_ctx, ctx, seq)
    lane = lax.broadcasted_iota(jnp.int32, (1, POOL_WIDTH), 1)
    acc = a
    lo, hi = 0, 1
    pooled = jnp.zeros_like(a)
    for g, w in enumerate(POOL_WINDOWS):
        for d in list(range(-w // 2, lo)) + list(range(hi, w // 2)):
            acc = acc + shifted(d)
        lo, hi = -w // 2, w // 2
        cnt = (jnp.minimum(pos + w // 2, length) - jnp.maximum(pos - w // 2, 0)).astype(F32)
        pg = acc / cnt - a
        in_group = jnp.logical_and(lane >= g * POOL_GROUP_DIM, lane < (g + 1) * POOL_GROUP_DIM)
        pooled = jnp.where(in_group, pg, pooled)
    y = _dot(pooled.astype(BF16), w_ref[...]) * sc_ref[...]
    o_ref[...] = y.astype(BF16)


def _pool(a, w_bd, scale, ntb, seq, ctx):
    rows = a.shape[0]
    prev_spec, next_spec = _halo_specs(POOL_WIDTH, ntb)
    return pl.pallas_call(
        functools.partial(_pool_kernel, ntb=ntb, seq=seq, ctx=ctx),
        grid=(rows // TM,),
        in_specs=[_tok_spec(POOL_WIDTH), prev_spec, next_spec, _full_spec(w_bd.shape), _full_spec(scale.shape)],
        out_specs=_tok_spec(POOL_WIDTH),
        out_shape=jax.ShapeDtypeStruct((rows, POOL_WIDTH), BF16),
        compiler_params=_params(("parallel",)), name="pool",
    )(a, a, a, w_bd, scale)


def _finish_heads(acc, extra_den):
    lane = lax.broadcasted_iota(jnp.int32, (1, LANES), 1)
    den = jnp.where(lane < HEAD_DIM, pltpu.roll(acc, HEAD_DIM, 1) + extra_den, 1.0)
    return acc / den


def _sink_col(sink_ref, g, nrow):
    r = lax.broadcasted_iota(jnp.int32, (SWA_GROUP * nrow, 1), 0)
    col = jnp.zeros((SWA_GROUP * nrow, 1), F32)
    for t in range(SWA_GROUP):
        col = jnp.where(r // nrow == t, sink_ref[g * SWA_GROUP + t], col)
    return col


def _swa_kernel(sink_ref, q_ref, kp_ref, kc_ref, kn_ref, kx_ref, vp_ref, vc_ref, vn_ref, vx_ref, o_ref, *, nblk):
    i = pl.program_id(1)
    nq = SWA_GROUP * SWA_BLOCK
    qrow = lax.broadcasted_iota(jnp.int32, (nq, 1), 0) % SWA_BLOCK
    kcol = lax.broadcasted_iota(jnp.int32, (1, 3 * SWA_BLOCK), 1)
    valid = jnp.logical_and(kcol >= qrow, kcol <= qrow + 2 * SWA_WINDOW)
    blk = kcol // SWA_BLOCK
    valid = jnp.logical_and(valid, jnp.logical_or(blk != 0, i > 0))
    valid = jnp.logical_and(valid, jnp.logical_or(blk != 2, i < nblk - 1))
    valid = jnp.logical_and(valid, i < nblk)
    for g in range(SWA_KV_HEADS):
        gs = slice(LANES * g, LANES * (g + 1))
        qo = LANES * SWA_GROUP * g
        q = jnp.concatenate([q_ref[:, qo + LANES * t:qo + LANES * (t + 1)] for t in range(SWA_GROUP)], axis=0)
        kb = jnp.concatenate([kp_ref[:, gs], kc_ref[:, gs], kn_ref[:, gs]], axis=0)
        vb = jnp.concatenate([vp_ref[:, gs], vc_ref[:, gs], vn_ref[:, gs]], axis=0)
        s_b = jnp.where(valid, _dot(q, kb, NT_DIMS), -jnp.inf)
        s_c = _dot(q, kx_ref[:, gs], NT_DIMS)
        sink = _sink_col(sink_ref, g, SWA_BLOCK)
        m = jnp.maximum(jnp.maximum(jnp.max(s_b, axis=-1, keepdims=True), jnp.max(s_c, axis=-1, keepdims=True)),
                        sink)
        p_b = jnp.exp(s_b - m).astype(BF16)
        p_c = jnp.exp(s_c - m).astype(BF16)
        acc = _dot(p_b, vb) + _dot(p_c, vx_ref[:, gs])
        o = _finish_heads(acc, jnp.exp(sink - m)).astype(BF16)
        for t in range(SWA_GROUP):
            o_ref[:, qo + LANES * t:qo + LANES * (t + 1)] = o[SWA_BLOCK * t:SWA_BLOCK * (t + 1)]


def _swa(q, k, v, sink, nb, seq, ctx):
    rows = q.shape[0]
    n = seq + ctx
    nblk = seq // SWA_BLOCK
    bps = n // SWA_BLOCK
    kw = SWA_KV_HEADS * LANES
    qw = SWA_Q_HEADS * LANES
    smem = pl.BlockSpec(memory_space=pltpu.SMEM)

    def kv_spec(off):
        return pl.BlockSpec((SWA_BLOCK, kw), lambda b, i: (b * bps + jnp.clip(i + off, 0, nblk - 1), 0))
    ctx_spec = pl.BlockSpec((ctx, kw), lambda b, i: (b * (n // ctx) + seq // ctx, 0))
    qo_spec = pl.BlockSpec((SWA_BLOCK, qw), lambda b, i: (b * bps + i, 0))
    return pl.pallas_call(
        functools.partial(_swa_kernel, nblk=nblk),
        grid=(nb, bps),
        in_specs=[smem, qo_spec, kv_spec(-1), kv_spec(0), kv_spec(1), ctx_spec,
                  kv_spec(-1), kv_spec(0), kv_spec(1), ctx_spec],
        out_specs=qo_spec,
        out_shape=jax.ShapeDtypeStruct((rows, qw), BF16),
        compiler_params=_params(("parallel", "parallel")), name="swa",
    )(sink, q, k, k, k, k, v, v, v, v)


MLA_TQ = 512
MLA_TK = 2048
LOG2E = math.log2(math.e)


def _mla_step(q, kc, vc, m, acc):
    s = _dot(q, kc, NT_DIMS)
    m_new = jnp.maximum(m, jnp.max(s, axis=-1, keepdims=True))
    alpha = jnp.exp2(m - m_new)
    p = jnp.exp2(s - m_new).astype(BF16)
    return m_new, alpha * acc + _dot(p, vc)


def _mla_lat_kernel(q_ref, k_ref, v_ref, o_ref, s_ref, *, seq, ctx, tk):
    q = q_ref[...]
    tq = q.shape[0]
    nchunk = seq // tk

    def scores(c):
        start = pl.multiple_of(c * tk, tk)
        return _dot(q, k_ref[pl.ds(start, tk), :], NT_DIMS)

    def half(c, slot, m, acc):
        s_ref[1 - slot] = scores(jnp.minimum(c + 1, nchunk - 1))
        s = s_ref[slot]
        start = pl.multiple_of(c * tk, tk)
        m_new = jnp.maximum(m, jnp.max(s, axis=-1, keepdims=True))
        alpha = jnp.exp2(m - m_new)
        p = jnp.exp2(s - m_new).astype(BF16)
        return m_new, alpha * acc + _dot(p, v_ref[pl.ds(start, tk), :])

    def body(c2, carry):
        m, acc = half(2 * c2, 0, *carry)
        return half(2 * c2 + 1, 1, m, acc)

    s_ref[0] = scores(0)
    init = (jnp.full((tq, 1), -jnp.inf, F32), jnp.zeros((tq, LANES), F32))
    m, acc = lax.fori_loop(0, nchunk // 2, body, init)
    _, acc = _mla_step(q, k_ref[pl.ds(seq, ctx), :], v_ref[pl.ds(seq, ctx), :], m, acc)
    o_ref[...] = _finish_heads(acc, 0.0).astype(BF16)


def _mla_ctx_kernel(q_ref, k_ref, v_ref, prev_ref, o_ref):
    del prev_ref
    q = q_ref[...]
    m0 = jnp.full((q.shape[0], 1), -jnp.inf, F32)
    _, acc = _mla_step(q, k_ref[...], v_ref[...], m0, jnp.zeros((q.shape[0], LANES), F32))
    o_ref[...] = _finish_heads(acc, 0.0).astype(BF16)


def _mla(q, k, v, nb, seq, ctx):
    rows = q.shape[0]
    n = seq + ctx
    q3, k3, v3 = (t.reshape(nb, n, MLA_HEADS * LANES) for t in (q, k, v))
    tq = min(MLA_TQ, seq)
    tk = min(MLA_TK, seq // 2)
    kv_spec = pl.BlockSpec((None, n, LANES), lambda b, h, i: (b, 0, h))
    qo_spec = pl.BlockSpec((None, tq, LANES), lambda b, h, i: (b, i, h))
    out = pl.pallas_call(
        functools.partial(_mla_lat_kernel, seq=seq, ctx=ctx, tk=tk),
        grid=(nb, MLA_HEADS, seq // tq),
        in_specs=[qo_spec, kv_spec, kv_spec],
        out_specs=qo_spec,
        out_shape=jax.ShapeDtypeStruct((nb, n, MLA_HEADS * LANES), BF16),
        scratch_shapes=[pltpu.VMEM((2, tq, tk), F32)],
        compiler_params=_params(("parallel", "parallel", "parallel")), name="mla_latent",
    )(q3, k3, v3)
    cx_spec = pl.BlockSpec((None, ctx, LANES), lambda b, h: (b, seq // ctx, h))
    out = pl.pallas_call(
        _mla_ctx_kernel,
        grid=(nb, MLA_HEADS),
        in_specs=[cx_spec, cx_spec, cx_spec, pl.BlockSpec(memory_space=pl.ANY)],
        out_specs=cx_spec,
        out_shape=jax.ShapeDtypeStruct((nb, n, MLA_HEADS * LANES), BF16),
        input_output_aliases={3: 0},
        compiler_params=_params(("parallel", "parallel")), name="mla_context",
    )(q3, k3, v3, out)
    return out.reshape(rows, MLA_HEADS * LANES)


def _hy_pre_kernel(u_ref, up_ref, un_ref, cw_ref, cb_ref, skip_ref, x0_ref, zs_ref, z_ref, *, ntb):
    i = pl.program_id(0)
    prev_ok, next_ok = _halo_valid(i, ntb)
    u = u_ref[...]
    r = lax.broadcasted_iota(jnp.int32, (TM, 1), 0)
    up_row = jnp.where(prev_ok, up_ref[7:8, :], 0.0)
    un_row = jnp.where(next_ok, un_ref[0:1, :], 0.0)
    um1 = jnp.where(r == 0, up_row, pltpu.roll(u, 1, 0))
    up1 = jnp.where(r == TM - 1, un_row, pltpu.roll(u, TM - 1, 0))
    uc = um1 * cw_ref[0:1, :] + u * cw_ref[1:2, :] + up1 * cw_ref[2:3, :] + cb_ref[...]
    w = HYENA_WIDTH
    x0 = uc[:, :w]
    z = uc[:, 2 * w:] * uc[:, w:2 * w]
    x0_ref[...] = x0
    zs_ref[...] = x0 * (skip_ref[...] * z)
    z_ref[...] = z


def _hy_pre(u, conv_w, conv_b, skip, ntb):
    rows = u.shape[0]
    w3 = 3 * HYENA_WIDTH
    prev_spec, next_spec = _halo_specs(w3, ntb)
    w = HYENA_WIDTH
    return pl.pallas_call(
        functools.partial(_hy_pre_kernel, ntb=ntb),
        grid=(rows // TM,),
        in_specs=[_tok_spec(w3), prev_spec, next_spec, _full_spec(conv_w.shape), _full_spec(conv_b.shape),
                  _full_spec(skip.shape)],
        out_specs=[_tok_spec(w), _tok_spec(w), _tok_spec(w)],
        out_shape=[jax.ShapeDtypeStruct((rows, w), F32)] * 3,
        compiler_params=_params(("parallel",)), name="hyena_pre",
    )(u, u, u, conv_w, conv_b, skip)


HIGHEST = lax.Precision.HIGHEST


def _filter_kernel(fr_ref, w1_ref, b1_ref, w2_ref, b2_ref, w3_ref, f0_ref, f1_ref, dl_ref, h_ref, n_ref,
                   *, length, tf):
    i = pl.program_id(0)
    pos = (i * tf + lax.broadcasted_iota(jnp.int32, (tf, 1), 0)).astype(F32)
    t = pos / max(length - 1, 1)
    lane = lax.broadcasted_iota(jnp.int32, (1, LANES), 1)
    bands = (HYENA_EMB - 1) // 2
    ang = (2 * math.pi / length) * pos * fr_ref[...]
    emb = jnp.where(lane == 0, t,
                    jnp.where(lane <= bands, jnp.cos(ang), jnp.where(lane <= 2 * bands, -jnp.sin(ang), 0.0)))
    h = jnp.sin(f0_ref[...] * (_dot(emb, w1_ref[...], precision=HIGHEST) + b1_ref[...]))
    h = jnp.sin(f1_ref[...] * (_dot(h, w2_ref[...], precision=HIGHEST) + b2_ref[...]))
    h = _dot(h, w3_ref[...], precision=HIGHEST)
    decay = jnp.exp(-t * dl_ref[...])
    hf = h[:, :HYENA_WIDTH] * decay
    hb = jnp.where(pos == 0.0, 0.0, h[:, HYENA_WIDTH:] * decay)
    h_ref[:, :HYENA_WIDTH] = hf
    h_ref[:, HYENA_WIDTH:] = hb

    @pl.when(i == 0)
    def _():
        n_ref[...] = jnp.zeros_like(n_ref)
    colsum = jnp.concatenate([jnp.sum(jnp.abs(hf), axis=0, keepdims=True),
                              jnp.sum(jnp.abs(hb), axis=0, keepdims=True)], axis=1)
    n_ref[...] += jnp.broadcast_to(colsum, n_ref.shape)


def _hyena_filter(length, fr, w1, b1, w2, b2, w3, f0, f1, deltas):
    tf = min(TM, length)
    consts = [fr, w1, b1, w2, b2, w3, f0, f1, deltas]
    return pl.pallas_call(
        functools.partial(_filter_kernel, length=length, tf=tf),
        grid=(length // tf,),
        in_specs=[_full_spec(c.shape) for c in consts],
        out_specs=[pl.BlockSpec((tf, 2 * HYENA_WIDTH), lambda i: (i, 0)),
                   pl.BlockSpec((8, 2 * HYENA_WIDTH), lambda i: (0, 0))],
        out_shape=[jax.ShapeDtypeStruct((length, 2 * HYENA_WIDTH), F32),
                   jax.ShapeDtypeStruct((8, 2 * HYENA_WIDTH), F32)],
        compiler_params=_params(("arbitrary",)), name="hyena_filter",
    )(*consts)


DFT_KG = 4
DFT_J = 8
DFT_WL = 512


def _dft_fwd_kernel(f_ref, x_ref, ar_ref, ai_ref):
    n1h, jj, wl = x_ref.shape
    n1 = ar_ref.shape[0]
    xs = x_ref[...].reshape(n1h * jj, wl).astype(BF16)
    ck = min(1024, n1 * jj)
    per = ck // jj
    for c in range(2 * n1 * jj // ck):
        r = _dot(f_ref[c * ck:(c + 1) * ck, :], xs).reshape(per, jj, wl)
        k0 = c * per
        if k0 < n1:
            ar_ref[k0:k0 + per] = r
        else:
            ai_ref[k0 - n1:k0 - n1 + per] = r


def _dft_fwd(f1k, x4):
    g, _, n2, w = x4.shape
    n1 = f1k.shape[0] // (2 * DFT_J)
    wl = min(DFT_WL, w)
    out_spec = pl.BlockSpec((None, n1, DFT_J, wl), lambda b, j, l: (b, 0, j, l))
    return pl.pallas_call(
        _dft_fwd_kernel,
        grid=(g, n2 // DFT_J, w // wl),
        in_specs=[pl.BlockSpec(f1k.shape, lambda b, j, l: (0, 0)),
                  pl.BlockSpec((None, n1 // 2, DFT_J, wl), lambda b, j, l: (b, 0, j, l))],
        out_specs=[out_spec, out_spec],
        out_shape=[jax.ShapeDtypeStruct((g, n1, n2, w), F32)] * 2,
        compiler_params=_params(("parallel", "parallel", "parallel")), name="hyena_dft_outer",
    )(f1k, x4)


def _spec_filter_kernel(m_ref, ar_ref, ai_ref, n_ref, gr_ref, gi_ref, *, inv_n):
    w = HYENA_WIDTH
    nrm = n_ref[0:1, :w] + n_ref[0:1, w:]
    inv = inv_n / nrm
    for kk in range(m_ref.shape[0]):
        x = jnp.concatenate([ar_ref[kk], ai_ref[kk]], axis=0).astype(BF16)
        y = _dot(m_ref[kk], x)
        n2 = y.shape[0] // 2
        gr_ref[kk] = (y[:n2, :w] + y[:n2, w:]) * inv
        gi_ref[kk] = (y[n2:, :w] - y[n2:, w:]) * inv


def _spec_kernel(m_ref, ar_ref, ai_ref, gr_ref, gi_ref, cr_ref, ci_ref):
    for kk in range(m_ref.shape[0]):
        mk = m_ref[kk]
        gr, gi = gr_ref[kk], gi_ref[kk]
        for b in range(ar_ref.shape[0]):
            x = jnp.concatenate([ar_ref[b, kk], ai_ref[b, kk]], axis=0).astype(BF16)
            y = _dot(mk, x)
            n2 = y.shape[0] // 2
            yr, yi = y[:n2], y[n2:]
            p = jnp.concatenate([yr * gr - yi * gi, yr * gi + yi * gr], axis=0).astype(BF16)
            c = _dot(mk, p, TN_DIMS)
            cr_ref[b, kk] = c[:n2]
            ci_ref[b, kk] = c[n2:]


def _spec_filter(mtab, ar, ai, nrm, n_total):
    n1 = mtab.shape[0]
    kg = min(DFT_KG, n1)
    w2 = ar.shape[-1]
    a_spec = pl.BlockSpec((kg, DFT_N2, w2), lambda j: (j, 0, 0))
    g_spec = pl.BlockSpec((kg, DFT_N2, HYENA_WIDTH), lambda j: (j, 0, 0))
    return pl.pallas_call(
        functools.partial(_spec_filter_kernel, inv_n=1.0 / n_total),
        grid=(n1 // kg,),
        in_specs=[pl.BlockSpec((kg, 2 * DFT_N2, 2 * DFT_N2), lambda j: (j, 0, 0)), a_spec, a_spec,
                  _full_spec(nrm.shape)],
        out_specs=[g_spec, g_spec],
        out_shape=[jax.ShapeDtypeStruct((n1, DFT_N2, HYENA_WIDTH), F32)] * 2,
        compiler_params=_params(("parallel",)), name="hyena_filter_spectrum",
    )(mtab, ar, ai, nrm)


def _spec(mtab, ar, ai, gr, gi):
    nb, n1 = ar.shape[0], ar.shape[1]
    kg = min(DFT_KG, n1)
    w = HYENA_WIDTH
    a_spec = pl.BlockSpec((nb, kg, DFT_N2, w), lambda j: (0, j, 0, 0))
    g_spec = pl.BlockSpec((kg, DFT_N2, w), lambda j: (j, 0, 0))
    return pl.pallas_call(
        _spec_kernel,
        grid=(n1 // kg,),
        in_specs=[pl.BlockSpec((kg, 2 * DFT_N2, 2 * DFT_N2), lambda j: (j, 0, 0)), a_spec, a_spec, g_spec, g_spec],
        out_specs=[a_spec, a_spec],
        out_shape=[jax.ShapeDtypeStruct(ar.shape, F32)] * 2,
        compiler_params=_params(("parallel",)), name="hyena_spectrum",
    )(mtab, ar, ai, gr, gi)


def _dft_inv_kernel(f_ref, cr_ref, ci_ref, x0_ref, zs_ref, o_ref):
    n1, jj, wl = cr_ref.shape
    c = jnp.concatenate([cr_ref[...].reshape(n1 * jj, wl), ci_ref[...].reshape(n1 * jj, wl)], axis=0).astype(BF16)
    y = _dot(f_ref[...], c).reshape(x0_ref.shape)
    o_ref[...] = x0_ref[...] * y + zs_ref[...]


def _dft_inv(finvk, cr, ci, x0v, zsv):
    nb, n1, n2, w = cr.shape
    c_spec = pl.BlockSpec((None, n1, DFT_J, w), lambda b, j: (b, 0, j, 0))
    t_spec = pl.BlockSpec((None, n1 // 2, DFT_J, w), lambda b, j: (b, 0, j, 0))
    return pl.pallas_call(
        _dft_inv_kernel,
        grid=(nb, n2 // DFT_J),
        in_specs=[pl.BlockSpec(finvk.shape, lambda b, j: (0, 0)), c_spec, c_spec, t_spec, t_spec],
        out_specs=t_spec,
        out_shape=jax.ShapeDtypeStruct(x0v.shape, F32),
        compiler_params=_params(("parallel", "parallel")), name="hyena_idft_outer",
    )(finvk, cr, ci, x0v, zsv)


def _hy_ctx_kernel(fc_ref, fi_ref, z_ref, h_ref, x0_ref, zs_ref, prev_ref, o_ref, *, n_total):
    del prev_ref
    w = HYENA_WIDTH
    h = h_ref[...]
    nk = fc_ref.shape[0] // 2
    nrm = jnp.sum(jnp.abs(h[:, :w]), axis=0, keepdims=True) + jnp.sum(jnp.abs(h[:, w:]), axis=0, keepdims=True)
    inv = (1.0 / n_total) / nrm
    hs = _dot(fc_ref[...], h.astype(BF16))
    gr = (hs[:nk, :w] + hs[:nk, w:]) * inv
    gi = (hs[nk:, :w] - hs[nk:, w:]) * inv
    zsp = _dot(fc_ref[...], z_ref[...].astype(BF16))
    zr, zi = zsp[:nk], zsp[nk:]
    p = jnp.concatenate([zr * gr - zi * gi, zr * gi + zi * gr], axis=0).astype(BF16)
    y = _dot(fi_ref[...], p)
    o_ref[...] = x0_ref[...] * y + zs_ref[...]


def _hy_ctx(fc, fi, z, hcat, x0, zs, out, nb, seq, ctx):
    n = seq + ctx
    w = HYENA_WIDTH
    row_spec = pl.BlockSpec((ctx, w), lambda b: (b * (n // ctx) + seq // ctx, 0))
    return pl.pallas_call(
        functools.partial(_hy_ctx_kernel, n_total=2 * ctx),
        grid=(nb,),
        in_specs=[_full_spec(fc.shape), _full_spec(fi.shape), row_spec, _full_spec(hcat.shape), row_spec, row_spec,
                  pl.BlockSpec(memory_space=pl.ANY)],
        out_specs=row_spec,
        out_shape=jax.ShapeDtypeStruct(out.shape, F32),
        input_output_aliases={6: 0},
        compiler_params=_params(("parallel",)), name="hyena_context",
    )(fc, fi, z, hcat, x0, zs, out)


def _dft_tables(seq, ctx):
    n = 2 * seq
    n1 = n // DFT_N2
    two_pi = 2.0 * math.pi

    def cs(num, den):
        ang = two_pi * (num % den).astype(F32) / den
        return jnp.cos(ang), jnp.sin(ang)
    k1 = jnp.arange(n1, dtype=jnp.int32)
    big = jnp.arange(2 * n1 * DFT_J, dtype=jnp.int32)
    small = jnp.arange(n1 // 2 * DFT_J, dtype=jnp.int32)
    kk, jb = big // DFT_J, big % DFT_J
    mm, js = small // DFT_J, small % DFT_J
    c, s = cs((kk % n1)[:, None] * mm[None, :], n1)
    f1k = jnp.where(jb[:, None] == js[None, :], jnp.where(kk[:, None] < n1, c, -s), 0.0).astype(BF16)
    finvk = f1k.T
    k2 = jnp.arange(DFT_N2, dtype=jnp.int32)
    phase = k2[None, None, :] * (k1[:, None, None] + n1 * k2[None, :, None])
    c, s = cs(phase, n)
    mtab = jnp.concatenate([jnp.concatenate([c, s], axis=2), jnp.concatenate([-s, c], axis=2)], axis=1)
    nc = 2 * ctx
    kk = jnp.arange(nc, dtype=jnp.int32)
    c, s = cs(kk[:, None] * kk[None, :ctx], nc)
    fc = jnp.concatenate([c, -s], axis=0).astype(BF16)
    fi = jnp.concatenate([c.T, -s.T], axis=1).astype(BF16)
    return f1k, finvk, mtab.astype(BF16), fc, fi


N_CLASSES = N_GROUPS * 6
CLS_ROWS = 32
PAIR_LO = (0, 0, 0, 1, 1, 2)
PAIR_HI = (1, 2, 3, 2, 3, 3)
HEXT = LANES


def _post_kernel(a1_ref, a2_ref, w1_ref, w2_ref, x_ref, mod_ref, g_ref, rw_ref, rb_ref,
                 xo_ref, hx_ref, meta_ref, cnt_ref, carry_ref):
    step = pl.program_id(0)

    @pl.when(step == 0)
    def _():
        carry_ref[...] = jnp.zeros_like(carry_ref)
    ml = _dot(a1_ref[...].astype(BF16), w1_ref[...]) + _dot(a2_ref[...], w2_ref[...])
    x = x_ref[...] + mod_ref[2:3, :] * ml
    xo_ref[...] = x
    hf = _modulate(x, g_ref[...], mod_ref[3:4, :], mod_ref[4:5, :])
    d = hf.shape[1]
    hx_ref[:, :d] = hf
    h = hf.astype(BF16)
    scores = jax.nn.sigmoid(_dot(rw_ref[...], h, NT_DIMS))
    biased = scores + rb_ref[...]
    sc = [scores[e:e + 1, :] for e in range(N_EXPERTS)]
    bi = [biased[e:e + 1, :] for e in range(N_EXPERTS)]
    best = None
    sel = None
    for g in range(N_GROUPS):
        v0, v1, v2, v3 = bi[4 * g:4 * g + 4]
        top1 = jnp.maximum(jnp.maximum(v0, v1), jnp.maximum(v2, v3))
        top2 = jnp.maximum(jnp.maximum(jnp.minimum(v0, v1), jnp.minimum(v2, v3)),
                           jnp.minimum(jnp.maximum(v0, v1), jnp.maximum(v2, v3)))
        gs = top1 + top2
        if g == 0:
            best, sel = gs, jnp.zeros_like(gs, dtype=jnp.int32)
        else:
            upd = gs > best
            best = jnp.where(upd, gs, best)
            sel = jnp.where(upd, g, sel)

    def pick(vals, j):
        out = vals[j]
        for g in range(1, N_GROUPS):
            out = jnp.where(sel == g, vals[4 * g + j], out)
        return out
    b = [pick(bi, j) for j in range(EXPERTS_PER_GROUP)]
    s = [pick(sc, j) for j in range(EXPERTS_PER_GROUP)]
    i1 = jnp.zeros_like(sel)
    m1 = b[0]
    for j in range(1, EXPERTS_PER_GROUP):
        upd = b[j] > m1
        m1 = jnp.where(upd, b[j], m1)
        i1 = jnp.where(upd, j, i1)
    i2 = jnp.full_like(sel, -1)
    m2 = jnp.full_like(m1, -jnp.inf)
    for j in range(EXPERTS_PER_GROUP):
        upd = jnp.logical_and(i1 != j, b[j] > m2)
        m2 = jnp.where(upd, b[j], m2)
        i2 = jnp.where(upd, j, i2)
    w1 = s[0]
    w2 = s[0]
    for j in range(1, EXPERTS_PER_GROUP):
        w1 = jnp.where(i1 == j, s[j], w1)
        w2 = jnp.where(i2 == j, s[j], w2)
    tot = w1 + w2
    first_lo = i1 < i2
    lo = jnp.minimum(i1, i2)
    hi = jnp.maximum(i1, i2)
    cls = sel * 6 + jnp.where(lo == 0, 0, jnp.where(lo == 1, 3, 5)) + hi - lo - 1
    w_lo = jnp.where(first_lo, w1, w2) / tot
    w_hi = jnp.where(first_lo, w2, w1) / tot
    lrow = lax.broadcasted_iota(jnp.int32, (LANES, 1), 0)
    wt = jnp.where(lrow == 0, w_lo, jnp.where(lrow == 1, w_hi, 0.0))
    hx_ref[:, d:] = wt.T
    tm = cls.shape[1]
    crow = lax.broadcasted_iota(jnp.int32, (CLS_ROWS, 1), 0)
    onehot = crow == cls
    tri = (lax.broadcasted_iota(jnp.int32, (tm, tm), 0) <= lax.broadcasted_iota(jnp.int32, (tm, tm), 1))
    cum = _dot(onehot.astype(BF16), tri.astype(BF16))
    carry = carry_ref[...]
    rank = jnp.sum(jnp.where(onehot, carry[:, 0:1] + cum, 0.0), axis=0, keepdims=True) - 1.0
    mrow = lax.broadcasted_iota(jnp.int32, (8, 1), 0)
    meta_ref[...] = jnp.where(mrow == 0, cls, jnp.where(mrow == 1, rank.astype(jnp.int32), 0))
    carry = carry + cum[:, tm - 1:tm]
    carry_ref[...] = carry
    cnt_ref[...] = carry


def _post(a1, a2, w1, w2, x, mod, g, rw, rb, ntb):
    rows, d = x.shape
    row = _mod_row(ntb)
    mod_spec = pl.BlockSpec((None, 8, d), lambda i: (row(i), 0, 0))
    nt = rows // TM
    return pl.pallas_call(
        _post_kernel,
        grid=(nt,),
        in_specs=[_tok_spec(a1.shape[1]), _tok_spec(a2.shape[1]), _full_spec(w1.shape), _full_spec(w2.shape),
                  _tok_spec(d), mod_spec, _full_spec(g.shape), _full_spec(rw.shape), _full_spec(rb.shape)],
        out_specs=[_tok_spec(d), _tok_spec(d + HEXT), pl.BlockSpec((None, 8, TM), lambda i: (i, 0, 0)),
                   pl.BlockSpec((CLS_ROWS, LANES), lambda i: (0, 0))],
        out_shape=[jax.ShapeDtypeStruct((rows, d), F32), jax.ShapeDtypeStruct((rows, d + HEXT), F32),
                   jax.ShapeDtypeStruct((nt, 8, TM), jnp.int32), jax.ShapeDtypeStruct((CLS_ROWS, LANES), F32)],
        scratch_shapes=[pltpu.VMEM((CLS_ROWS, LANES), F32)],
        compiler_params=_params(("arbitrary",)), name="post_mixer",
    )(a1, a2, w1, w2, x, mod, g, rw, rb)


MOE_TM = 256
MOE_TD = 512
MOE_UNROLL = 8


def _route_plan(meta, cnt, rows):
    counts = cnt[:N_CLASSES, 0].astype(jnp.int32)
    padded = ((counts + MOE_TM - 1) // MOE_TM) * MOE_TM
    ends = jnp.cumsum(padded)
    offs = ends - padded
    cls = meta[:, 0, :].reshape(rows)
    rank = meta[:, 1, :].reshape(rows)
    slot = offs[cls] + rank
    ntiles = rows // MOE_TM + N_CLASSES
    starts = jnp.arange(ntiles, dtype=jnp.int32) * MOE_TM
    tcls = jnp.sum((ends[None, :] <= starts[:, None]).astype(jnp.int32), axis=1)
    valid = tcls < N_CLASSES
    nvalid = jnp.sum(valid.astype(jnp.int32))
    last = tcls[jnp.maximum(nvalid - 1, 0)]
    tcls = jnp.where(valid, tcls, jnp.minimum(last, N_CLASSES - 1))
    grp, pair = tcls // 6, tcls % 6
    ea = grp * EXPERTS_PER_GROUP + jnp.asarray(PAIR_LO, jnp.int32)[pair]
    eb = grp * EXPERTS_PER_GROUP + jnp.asarray(PAIR_HI, jnp.int32)[pair]
    return slot, ea, eb, valid.astype(jnp.int32), ntiles


def _row_copy(src, dst, sem, s_row, d_row):
    return pltpu.make_async_copy(src.at[pl.ds(s_row, 1), :], dst.at[pl.ds(d_row, 1), :], sem)


def _dispatch_kernel(slot_ref, hx_ref, init_ref, xs_ref, sem):
    del init_ref
    td = hx_ref.shape[0]

    def issue(t, c):
        _row_copy(hx_ref, xs_ref, sem, t, slot_ref[0, t]).start()
        return c
    lax.fori_loop(0, td, issue, 0, unroll=MOE_UNROLL)
    pltpu.make_async_copy(hx_ref, xs_ref.at[pl.ds(0, td), :], sem).wait()


def _dispatch(hx, slot, nrows_sorted):
    rows, width = hx.shape
    nsteps = rows // MOE_TD
    init = jnp.zeros((nrows_sorted, width), F32)
    return pl.pallas_call(
        _dispatch_kernel,
        grid=(nsteps,),
        in_specs=[pl.BlockSpec((None, 1, MOE_TD), lambda i: (i, 0, 0), memory_space=pltpu.SMEM),
                  pl.BlockSpec((MOE_TD, width), lambda i: (i, 0)),
                  pl.BlockSpec(memory_space=pl.ANY)],
        out_specs=pl.BlockSpec(memory_space=pl.ANY),
        out_shape=jax.ShapeDtypeStruct((nrows_sorted, width), F32),
        scratch_shapes=[pltpu.SemaphoreType.DMA(())],
        input_output_aliases={2: 0},
        compiler_params=_params(("arbitrary",)), name="moe_dispatch",
    )(slot.reshape(nsteps, 1, MOE_TD), hx, init)


def _experts_kernel(ea_ref, eb_ref, valid_ref, xs_ref, wga_ref, wua_ref, wda_ref, wgb_ref, wub_ref, wdb_ref, ys_ref):
    del ea_ref, eb_ref
    j = pl.program_id(0)

    @pl.when(valid_ref[j] == 0)
    def _():
        ys_ref[...] = jnp.zeros_like(ys_ref)

    @pl.when(valid_ref[j] != 0)
    def _():
        d = ys_ref.shape[1]
        x = xs_ref[:, :d].astype(BF16)

        def ffn(wg_ref, wu_ref, wd_ref, w):
            gate = _dot(x, wg_ref[...])
            up = _dot(x, wu_ref[...])
            a = (gate * jax.nn.sigmoid(gate) * up * w).astype(BF16)
            return _dot(a, wd_ref[...])
        ys_ref[...] = (ffn(wga_ref, wua_ref, wda_ref, xs_ref[:, d:d + 1])
                       + ffn(wgb_ref, wub_ref, wdb_ref, xs_ref[:, d + 1:d + 2]))


def _experts(xs, ea, eb, valid, wg, wu, wd, layer, ntiles):
    d = wg.shape[2]
    ff = wg.shape[3]
    gu_a = pl.BlockSpec((None, None, d, ff), lambda j, ea, eb, v: (layer, ea[j], 0, 0))
    gu_b = pl.BlockSpec((None, None, d, ff), lambda j, ea, eb, v: (layer, eb[j], 0, 0))
    dn_a = pl.BlockSpec((None, None, ff, d), lambda j, ea, eb, v: (layer, ea[j], 0, 0))
    dn_b = pl.BlockSpec((None, None, ff, d), lambda j, ea, eb, v: (layer, eb[j], 0, 0))
    return pl.pallas_call(
        _experts_kernel,
        grid_spec=pltpu.PrefetchScalarGridSpec(
            num_scalar_prefetch=3, grid=(ntiles,),
            in_specs=[pl.BlockSpec((MOE_TM, xs.shape[1]), lambda j, ea, eb, v: (j, 0)),
                      gu_a, gu_a, dn_a, gu_b, gu_b, dn_b],
            out_specs=pl.BlockSpec((MOE_TM, d), lambda j, ea, eb, v: (j, 0))),
        out_shape=jax.ShapeDtypeStruct((ntiles * MOE_TM, d), F32),
        compiler_params=_params(("arbitrary",)), name="moe_experts",
    )(ea, eb, valid, xs, wg, wu, wd, wg, wu, wd)


def _undispatch_kernel(slot_ref, ys_ref, y_ref, sem):
    td = y_ref.shape[0]

    def issue(t, c):
        _row_copy(ys_ref, y_ref, sem, slot_ref[0, t], t).start()
        return c
    lax.fori_loop(0, td, issue, 0, unroll=MOE_UNROLL)
    pltpu.make_async_copy(ys_ref.at[pl.ds(0, td), :], y_ref, sem).wait()


def _undispatch(ys, slot, rows):
    d = ys.shape[1]
    nsteps = rows // MOE_TD
    return pl.pallas_call(
        _undispatch_kernel,
        grid=(nsteps,),
        in_specs=[pl.BlockSpec((None, 1, MOE_TD), lambda i: (i, 0, 0), memory_space=pltpu.SMEM),
                  pl.BlockSpec(memory_space=pl.ANY)],
        out_specs=pl.BlockSpec((MOE_TD, d), lambda i: (i, 0)),
        out_shape=jax.ShapeDtypeStruct((rows, d), F32),
        scratch_shapes=[pltpu.SemaphoreType.DMA(())],
        compiler_params=_params(("arbitrary",)), name="moe_undispatch",
    )(slot.reshape(nsteps, 1, MOE_TD), ys)


def _moe(hx, meta, cnt, wg, wu, wd, layer):
    rows = hx.shape[0]
    slot, ea, eb, valid, ntiles = _route_plan(meta, cnt, rows)
    xs = _dispatch(hx, slot, ntiles * MOE_TM)
    ys = _experts(xs, ea, eb, valid, wg, wu, wd, layer, ntiles)
    return _undispatch(ys, slot, rows)


def _final_kernel(x_ref, y_ref, mod_ref, o_ref):
    o_ref[...] = x_ref[...] + mod_ref[5:6, :] * y_ref[...]


def _final(x, y, mod, nb, seq, ctx):
    d = x.shape[1]
    ntb = (seq + ctx) // TM
    nlt = seq // TM
    tok = pl.BlockSpec((TM, d), lambda b, j: (b * ntb + j, 0))
    return pl.pallas_call(
        _final_kernel,
        grid=(nb, nlt),
        in_specs=[tok, tok, pl.BlockSpec((None, 8, d), lambda b, j: (b, 0, 0))],
        out_specs=pl.BlockSpec((None, TM, d), lambda b, j: (b, j, 0)),
        out_shape=jax.ShapeDtypeStruct((nb, seq, d), F32),
        compiler_params=_params(("parallel", "parallel")), name="final_residual",
    )(x, y, mod)


def _pad_heads(w, heads, dim, axis):
    shp = w.shape
    w = w.reshape(shp[:axis] + (heads, dim) + shp[axis + 1:])
    pad = [(0, 0)] * w.ndim
    pad[axis + 1] = (0, LANES - dim)
    w = jnp.pad(w, pad)
    return w.reshape(shp[:axis] + (heads * LANES,) + shp[axis + 1:])


def _pad_vec(v, mult=1.0):
    return jnp.pad(v.astype(F32) * mult, (0, LANES - v.shape[0])).reshape(1, LANES)


def _rope_tables(seq, ctx, d_rot, off):
    rows = seq // GRID_W
    row = jnp.repeat(jnp.arange(rows), GRID_W).astype(F32)
    col = jnp.tile(jnp.arange(GRID_W), rows).astype(F32)
    n_freq = d_rot // 4
    inv = ROPE_BASE ** (-jnp.arange(n_freq, dtype=F32) / n_freq)
    ang = jnp.concatenate([row[:, None] * inv, col[:, None] * inv], axis=-1)
    cos, sin = jnp.cos(ang), jnp.sin(ang)
    half = d_rot // 2

    def z(r, w):
        return jnp.zeros((r, w), F32)
    rest = LANES - off - 2 * half
    cos_l = jnp.concatenate([jnp.ones((seq, off), F32), cos, cos, z(seq, rest)], axis=1)
    sa_l = jnp.concatenate([z(seq, off + half), sin, z(seq, rest)], axis=1)
    sb_l = jnp.concatenate([z(seq, off), -sin, z(seq, half + rest)], axis=1)
    cos_c = jnp.concatenate([jnp.ones((ctx, off + 2 * half), F32), z(ctx, rest)], axis=1)
    return (jnp.concatenate([cos_l, cos_c], axis=0), jnp.concatenate([sa_l, z(ctx, LANES)], axis=0),
            jnp.concatenate([sb_l, z(ctx, LANES)], axis=0))


def _rot_cols(w):
    half = MLA_ROPE // 2
    w3 = w.reshape(w.shape[0], MLA_HEADS, LANES)
    z = jnp.zeros_like(w3)
    rot = jnp.concatenate([z[..., :MLA_NOPE], -w3[..., MLA_NOPE + half:MLA_NOPE + 2 * half],
                           w3[..., MLA_NOPE:MLA_NOPE + half], z[..., MLA_NOPE + 2 * half:]], axis=-1)
    return rot.reshape(w.shape)


def _gain_tables(tabs, gain):
    cos_t, sa_t, sb_t = tabs
    half = MLA_ROPE // 2
    return (cos_t * gain, sa_t * jnp.roll(gain, half, axis=1) - sb_t * jnp.roll(gain, LANES - half, axis=1))


def _forward(x, c, ctx, c_ctx, ada_w, ada_b, norm_g, ev_w_in, ev_w_out, pool_w, pool_scale,
             swa_q_gain, swa_k_gain, swa_sink, od_w_in, od_w_out, hy_conv_w, hy_conv_b,
             hy_w1, hy_b1, hy_w2, hy_b2, hy_w3, hy_freq, hy_skip, mla_cq_gain, mla_ckv_gain,
             mla_w_uq, mla_w_ukv, mla_q_gain, mla_k_gain, router_w, router_b,
             moe_w_gate, moe_w_up, moe_w_down):
    nb, seq, d = x.shape
    nctx = ctx.shape[1]
    depth = ada_w.shape[0]
    assert nctx == TM and seq % (2 * TM) == 0 and seq % GRID_W == 0 and nb <= 2
    n = seq + nctx
    ntb = n // TM
    rows = nb * n

    cvec = jnp.concatenate([c, c_ctx[None, :], jnp.zeros((8 - nb - 1, d), F32)], axis=0)
    if nb == 1:
        cvec = jnp.concatenate([c, jnp.zeros((1, d), F32), c_ctx[None, :], jnp.zeros((5, d), F32)], axis=0)
    mod = _adaln(cvec, ada_w, ada_b)

    xs = jnp.concatenate([x, ctx], axis=1).reshape(rows, d)
    tabs_swa = _rope_tables(seq, nctx, HEAD_DIM, 0)
    tabs_mla = _rope_tables(seq, nctx, MLA_ROPE, MLA_NOPE)
    f1k, finvk, mtab, fc, fi = _dft_tables(seq, nctx)
    bands = (HYENA_EMB - 1) // 2
    frv = jnp.linspace(1e-4, bands - 1, bands, dtype=F32)
    fr = jnp.concatenate([jnp.zeros((1,), F32), frv, frv, jnp.zeros((LANES - 1 - 2 * bands,), F32)]).reshape(1, LANES)
    deltas = jnp.abs(jnp.linspace(math.log(HYENA_TARGET) / HYENA_FAST_DECAY,
                                  math.log(HYENA_TARGET) / HYENA_SLOW_DECAY, HYENA_WIDTH, dtype=F32)).reshape(1, -1)
    rw = jnp.transpose(router_w).astype(BF16)
    rb = router_b.astype(F32).reshape(N_EXPERTS, 1)

    wg16, wu16, wd16 = moe_w_gate.astype(BF16), moe_w_up.astype(BF16), moe_w_down.astype(BF16)

    prev = None
    y = None
    for layer in range(depth):
        i = layer // 2
        lmod = mod[layer]
        g1 = norm_g[layer, 0].reshape(1, d)
        g2 = norm_g[layer, 1].reshape(1, d)
        if layer % 2 == 0:
            w = ev_w_in[i]
            o1 = POOL_WIDTH
            o2 = o1 + SWA_Q_HEADS * HEAD_DIM
            o3 = o2 + SWA_KV_HEADS * HEAD_DIM
            w_in = jnp.concatenate([w[:, :o1], _pad_heads(w[:, o1:o2], SWA_Q_HEADS, HEAD_DIM, 1),
                                    _pad_heads(w[:, o2:o3], SWA_KV_HEADS, HEAD_DIM, 1),
                                    _pad_heads(w[:, o3:], SWA_KV_HEADS, HEAD_DIM, 1)], axis=1).astype(BF16)
            qg = _pad_vec(swa_q_gain[i], HEAD_DIM ** -0.5)
            kg = _pad_vec(swa_k_gain[i])
            xs, a, q, k, v = _even_in(xs, prev, lmod, g1, w_in, qg, kg, tabs_swa, ntb)
            w_bd = jnp.zeros((POOL_WIDTH, POOL_WIDTH), F32)
            for g in range(POOL_GROUPS):
                sl = slice(g * POOL_GROUP_DIM, (g + 1) * POOL_GROUP_DIM)
                w_bd = w_bd.at[sl, sl].set(pool_w[i, g])
            mix1 = _pool(a, w_bd.astype(BF16), pool_scale[i].reshape(1, -1), ntb, seq, nctx)
            mix2 = _swa(q, k, v, swa_sink[i].astype(F32), nb, seq, nctx)
            wo = ev_w_out[i]
            wo1 = wo[:POOL_WIDTH].astype(BF16)
            wo2 = _pad_heads(wo[POOL_WIDTH:], SWA_Q_HEADS, HEAD_DIM, 0).astype(BF16)
        else:
            w_in = jnp.pad(od_w_in[i], ((0, 0), (0, LANES - MLA_ROPE))).astype(BF16)
            wuq = _pad_heads(mla_w_uq[i], MLA_HEADS, MLA_QK, 1)
            wuq = jnp.concatenate([wuq, _rot_cols(wuq)], axis=1).astype(BF16)
            wukv = mla_w_ukv[i].reshape(MLA_KV_RANK, MLA_HEADS, MLA_NOPE + MLA_V)
            wuk = _pad_heads(wukv[:, :, :MLA_NOPE].reshape(MLA_KV_RANK, -1), MLA_HEADS, MLA_NOPE, 1).astype(BF16)
            wuv = _pad_heads(wukv[:, :, MLA_NOPE:].reshape(MLA_KV_RANK, -1), MLA_HEADS, MLA_V, 1).astype(BF16)
            qg = _pad_vec(mla_q_gain[i], MLA_QK ** -0.5 * LOG2E)
            kg = _pad_vec(mla_k_gain[i])
            xs, u, q, k, v = _odd_in(xs, prev, lmod, g1, w_in, mla_cq_gain[i].reshape(1, -1),
                                     mla_ckv_gain[i].reshape(1, -1), wuq, wuk, wuv,
                                     _gain_tables(tabs_mla, qg) + _gain_tables(tabs_mla, kg), ntb)
            x0, zs, z = _hy_pre(u, hy_conv_w[i], hy_conv_b[i].reshape(1, -1), hy_skip[i].reshape(1, -1), ntb)
            w1p = jnp.zeros((LANES, LANES), F32).at[:HYENA_EMB, :HYENA_HIDDEN].set(hy_w1[i])
            w2p = jnp.zeros((LANES, LANES), F32).at[:HYENA_HIDDEN, :HYENA_HIDDEN].set(hy_w2[i])
            w3p = jnp.zeros((LANES, 2 * HYENA_WIDTH), F32).at[:HYENA_HIDDEN].set(hy_w3[i])
            fparams = (fr, w1p, _pad_vec(hy_b1[i]), w2p, _pad_vec(hy_b2[i]), w3p,
                       _pad_vec(hy_freq[i, 0]), _pad_vec(hy_freq[i, 1]), deltas)
            hcat, nrm = _hyena_filter(seq, *fparams)
            hcat_c, _ = _hyena_filter(nctx, *fparams)
            n1 = 2 * seq // DFT_N2
            w = HYENA_WIDTH
            har, hai = _dft_fwd(f1k, hcat.reshape(1, n1 // 2, DFT_N2, 2 * w))
            gr, gi = _spec_filter(mtab, har[0], hai[0], nrm, 2 * seq)
            view = (nb, n // DFT_N2, DFT_N2, w)
            ar, ai = _dft_fwd(f1k, z.reshape(view))
            cr, ci = _spec(mtab, ar, ai, gr, gi)
            hy = _dft_inv(finvk, cr, ci, x0.reshape(view), zs.reshape(view))
            mix1 = _hy_ctx(fc, fi, z, hcat_c, x0, zs, hy.reshape(rows, w), nb, seq, nctx)
            mix2 = _mla(q, k, v, nb, seq, nctx)
            wo = od_w_out[i]
            wo1 = wo[:HYENA_WIDTH].astype(BF16)
            wo2 = _pad_heads(wo[HYENA_WIDTH:], MLA_HEADS, MLA_V, 0).astype(BF16)
        xs, hx, meta, cnt = _post(mix1, mix2, wo1, wo2, xs, lmod, g2, rw, rb, ntb)
        y = _moe(hx, meta, cnt, wg16, wu16, wd16, layer)
        prev = (y, lmod)
    return _final(xs, y, mod[depth - 1], nb, seq, nctx)


def kernel(x, c, ctx, c_ctx, ada_w, ada_b, norm_g, ev_w_in, ev_w_out, pool_w, pool_scale, swa_q_gain, swa_k_gain, swa_sink, od_w_in, od_w_out, hy_conv_w, hy_conv_b, hy_w1, hy_b1, hy_w2, hy_b2, hy_w3, hy_freq, hy_skip, mla_cq_gain, mla_ckv_gain, mla_w_uq, mla_w_ukv, mla_q_gain, mla_k_gain, router_w, router_b, moe_w_gate, moe_w_up, moe_w_down):
    return _forward(x, c, ctx, c_ctx, ada_w, ada_b, norm_g, ev_w_in, ev_w_out, pool_w, pool_scale,
                    swa_q_gain, swa_k_gain, swa_sink, od_w_in, od_w_out, hy_conv_w, hy_conv_b,
                    hy_w1, hy_b1, hy_w2, hy_b2, hy_w3, hy_freq, hy_skip, mla_cq_gain, mla_ckv_gain,
                    mla_w_uq, mla_w_ukv, mla_q_gain, mla_k_gain, router_w, router_b,
                    moe_w_gate, moe_w_up, moe_w_down)
```

```python
import functools
import math

import jax
import jax.numpy as jnp
from jax import lax
from jax.experimental import pallas as pl
from jax.experimental.pallas import tpu as pltpu

F32 = jnp.float32
BF16 = jnp.bfloat16

GRID_W = 64
HEAD_DIM = 64
ROPE_BASE = 10000.0
EPS = 1e-6
POOL_GROUPS = 4
POOL_GROUP_DIM = 64
POOL_WIDTH = POOL_GROUPS * POOL_GROUP_DIM
POOL_WINDOWS = (2, 4, 8, 16)
SWA_Q_HEADS = 12
SWA_KV_HEADS = 4
SWA_GROUP = SWA_Q_HEADS // SWA_KV_HEADS
SWA_WINDOW = 128
SWA_BLOCK = 128
HYENA_WIDTH = 512
HYENA_EMB = 33
HYENA_HIDDEN = 64
HYENA_FAST_DECAY = 0.3
HYENA_SLOW_DECAY = 1.5
HYENA_TARGET = 1e-2
MLA_HEADS = 8
MLA_NOPE = 64
MLA_ROPE = 32
MLA_QK = MLA_NOPE + MLA_ROPE
MLA_V = 64
MLA_Q_RANK = 256
MLA_KV_RANK = 128
N_EXPERTS = 16
N_GROUPS = 4
EXPERTS_PER_GROUP = N_EXPERTS // N_GROUPS
EXPERT_FF = 512

LANES = 128
TM = 256
DFT_N2 = 128
VMEM_LIMIT = 56 * 1024 * 1024

NT_DIMS = (((1,), (1,)), ((), ()))
TN_DIMS = (((0,), (0,)), ((), ()))


def _dot(a, b, dims=None, precision=None):
    if dims is None:
        return jnp.dot(a, b, preferred_element_type=F32, precision=precision)
    return lax.dot_general(a, b, dims, preferred_element_type=F32, precision=precision)


def _params(sem):
    return pltpu.CompilerParams(dimension_semantics=sem, vmem_limit_bytes=VMEM_LIMIT)


def _mod_row(ntb):
    def f(i):
        return jnp.where(i % ntb == ntb - 1, 2, i // ntb)
    return f


def _modulate(x, g, shift, scale):
    ms = jnp.mean(x * x, axis=-1, keepdims=True)
    return (x * lax.rsqrt(ms + EPS) * g) * (1.0 + scale) + shift


def _head_norm_rope(xh, real_dim, gain, cos, sa, sb, half):
    r = lax.rsqrt(jnp.sum(xh * xh, axis=-1, keepdims=True) * (1.0 / real_dim) + EPS)
    xn = xh * r * gain
    return xn * cos + pltpu.roll(xn, half, 1) * sa + pltpu.roll(xn, LANES - half, 1) * sb


def _adaln_kernel(c_ref, w_ref, b_ref, o_ref):
    c = c_ref[...]
    s = (c * jax.nn.sigmoid(c)).astype(BF16)
    o_ref[...] = _dot(s, w_ref[...].astype(BF16)) + b_ref[...]


def _adaln(cvec, ada_w, ada_b):
    depth, d, d6 = ada_w.shape
    nchunk = d6 // d
    out = pl.pallas_call(
        _adaln_kernel,
        grid=(depth, nchunk),
        in_specs=[pl.BlockSpec((8, d), lambda l, j: (0, 0)),
                  pl.BlockSpec((None, d, d), lambda l, j: (l, 0, j)),
                  pl.BlockSpec((None, 1, d), lambda l, j: (l, 0, j))],
        out_specs=pl.BlockSpec((None, None, 8, d), lambda l, j: (l, j, 0, 0)),
        out_shape=jax.ShapeDtypeStruct((depth, nchunk, 8, d), F32),
        compiler_params=_params(("parallel", "parallel")),
        name="adaln",
    )(cvec, ada_w, ada_b.reshape(depth, 1, d6))
    mod = jnp.transpose(out, (0, 2, 1, 3))
    return jnp.pad(mod, ((0, 0), (0, 0), (0, 8 - nchunk), (0, 0)))


def _even_in_kernel(*refs, has_prev):
    if has_prev:
        (x_ref, y_ref, pmod_ref, mod_ref, g_ref, w_ref, qg_ref, kg_ref, cos_ref, sa_ref, sb_ref,
         xo_ref, a_ref, q_ref, k_ref, v_ref) = refs
        x = x_ref[...] + pmod_ref[5:6, :] * y_ref[...]
        xo_ref[...] = x
    else:
        (x_ref, mod_ref, g_ref, w_ref, qg_ref, kg_ref, cos_ref, sa_ref, sb_ref,
         a_ref, q_ref, k_ref, v_ref) = refs
        x = x_ref[...]
    h = _modulate(x, g_ref[...], mod_ref[0:1, :], mod_ref[1:2, :]).astype(BF16)
    p = _dot(h, w_ref[...])
    a_ref[...] = p[:, :POOL_WIDTH]
    cos, sa, sb = cos_ref[...], sa_ref[...], sb_ref[...]
    o = POOL_WIDTH
    for hh in range(SWA_Q_HEADS):
        xh = p[:, o + LANES * hh:o + LANES * (hh + 1)]
        q_ref[:, LANES * hh:LANES * (hh + 1)] = _head_norm_rope(
            xh, HEAD_DIM, qg_ref[...], cos, sa, sb, HEAD_DIM // 2).astype(BF16)
    o += SWA_Q_HEADS * LANES
    for hh in range(SWA_KV_HEADS):
        xh = p[:, o + LANES * hh:o + LANES * (hh + 1)]
        k_ref[:, LANES * hh:LANES * (hh + 1)] = _head_norm_rope(
            xh, HEAD_DIM, kg_ref[...], cos, sa, sb, HEAD_DIM // 2).astype(BF16)
    o += SWA_KV_HEADS * LANES
    ones_hi = (lax.broadcasted_iota(jnp.int32, (1, LANES), 1) >= HEAD_DIM).astype(F32)
    for hh in range(SWA_KV_HEADS):
        vh = p[:, o + LANES * hh:o + LANES * (hh + 1)]
        v_ref[:, LANES * hh:LANES * (hh + 1)] = (vh + ones_hi).astype(BF16)


def _odd_in_kernel(x_ref, y_ref, pmod_ref, mod_ref, g_ref, w_ref, cqg_ref, ckvg_ref, wuq_ref, wuk_ref,
                   wuv_ref, cq_ref, sq_ref, ck_ref, sk_ref,
                   xo_ref, u_ref, q_ref, k_ref, v_ref):
    x = x_ref[...] + pmod_ref[5:6, :] * y_ref[...]
    xo_ref[...] = x
    h = _modulate(x, g_ref[...], mod_ref[0:1, :], mod_ref[1:2, :]).astype(BF16)
    p = _dot(h, w_ref[...])
    nu = 3 * HYENA_WIDTH
    u_ref[...] = p[:, :nu]
    cq = p[:, nu:nu + MLA_Q_RANK]
    ckv = p[:, nu + MLA_Q_RANK:nu + MLA_Q_RANK + MLA_KV_RANK]
    krb = p[:, nu + MLA_Q_RANK + MLA_KV_RANK:]
    cqn = (cq * lax.rsqrt(jnp.mean(cq * cq, axis=-1, keepdims=True) + EPS) * cqg_ref[...]).astype(BF16)
    ckvn = (ckv * lax.rsqrt(jnp.mean(ckv * ckv, axis=-1, keepdims=True) + EPS) * ckvg_ref[...]).astype(BF16)
    nh = MLA_HEADS * LANES
    qp = _dot(cqn, wuq_ref[...])
    kp = _dot(ckvn, wuk_ref[...])
    vp = _dot(ckvn, wuv_ref[...])
    half = MLA_ROPE // 2
    lane = lax.broadcasted_iota(jnp.int32, (1, LANES), 1)
    krp = pltpu.roll(krb, MLA_NOPE, 1)
    krot = (jnp.where(jnp.logical_and(lane >= MLA_NOPE + half, lane < MLA_NOPE + 2 * half),
                      pltpu.roll(krp, half, 1), 0.0)
            - jnp.where(jnp.logical_and(lane >= MLA_NOPE, lane < MLA_NOPE + half),
                        pltpu.roll(krp, LANES - half, 1), 0.0))
    cq_t, sq_t, ck_t, sk_t = cq_ref[...], sq_ref[...], ck_ref[...], sk_ref[...]
    ones_hi = (lane >= MLA_V).astype(F32)

    def norm(xh):
        return lax.rsqrt(jnp.sum(xh * xh, axis=-1, keepdims=True) * (1.0 / MLA_QK) + EPS)
    for hh in range(MLA_HEADS):
        sl = slice(LANES * hh, LANES * (hh + 1))
        xq = qp[:, sl]
        q_ref[:, sl] = ((xq * cq_t + qp[:, nh + LANES * hh:nh + LANES * (hh + 1)] * sq_t) * norm(xq)).astype(BF16)
        xk = kp[:, sl] + krp
        k_ref[:, sl] = ((xk * ck_t + krot * sk_t) * norm(xk)).astype(BF16)
        v_ref[:, sl] = (vp[:, sl] + ones_hi).astype(BF16)


def _tok_spec(width):
    return pl.BlockSpec((TM, width), lambda i: (i, 0))


def _full_spec(shape):
    nd = len(shape)
    return pl.BlockSpec(shape, lambda i: (0,) * nd)


def _even_in(x, prev, mod, g, w, qg, kg, tabs, ntb):
    rows, d = x.shape
    row = _mod_row(ntb)
    mod_spec = pl.BlockSpec((None, 8, d), lambda i: (row(i), 0, 0))
    tab_spec = pl.BlockSpec((TM, LANES), lambda i: (i % ntb, 0))
    nq, nk = SWA_Q_HEADS * LANES, SWA_KV_HEADS * LANES
    ins = [x]
    specs = [_tok_spec(d)]
    outs = []
    ospecs = []
    if prev is not None:
        y, pmod = prev
        ins += [y, pmod]
        specs += [_tok_spec(d), mod_spec]
        outs.append(jax.ShapeDtypeStruct((rows, d), F32))
        ospecs.append(_tok_spec(d))
    ins += [mod, g, w, qg, kg, *tabs]
    specs += [mod_spec, _full_spec(g.shape), _full_spec(w.shape), _full_spec(qg.shape), _full_spec(kg.shape),
              tab_spec, tab_spec, tab_spec]
    outs += [jax.ShapeDtypeStruct((rows, POOL_WIDTH), F32), jax.ShapeDtypeStruct((rows, nq), BF16),
             jax.ShapeDtypeStruct((rows, nk), BF16), jax.ShapeDtypeStruct((rows, nk), BF16)]
    ospecs += [_tok_spec(POOL_WIDTH), _tok_spec(nq), _tok_spec(nk), _tok_spec(nk)]
    res = pl.pallas_call(
        functools.partial(_even_in_kernel, has_prev=prev is not None),
        grid=(rows // TM,), in_specs=specs, out_specs=ospecs, out_shape=outs,
        compiler_params=_params(("parallel",)), name="even_in",
    )(*ins)
    if prev is None:
        return (x, *res)
    return res


def _odd_in(x, prev, mod, g, w, cqg, ckvg, wuq, wuk, wuv, tabs, ntb):
    rows, d = x.shape
    row = _mod_row(ntb)
    mod_spec = pl.BlockSpec((None, 8, d), lambda i: (row(i), 0, 0))
    tab_spec = pl.BlockSpec((TM, LANES), lambda i: (i % ntb, 0))
    y, pmod = prev
    nh = MLA_HEADS * LANES
    consts = [g, w, cqg, ckvg, wuq, wuk, wuv]
    return pl.pallas_call(
        _odd_in_kernel,
        grid=(rows // TM,),
        in_specs=[_tok_spec(d), _tok_spec(d), mod_spec, mod_spec] + [_full_spec(c.shape) for c in consts]
        + [tab_spec] * len(tabs),
        out_specs=[_tok_spec(d), _tok_spec(3 * HYENA_WIDTH), _tok_spec(nh), _tok_spec(nh), _tok_spec(nh)],
        out_shape=[jax.ShapeDtypeStruct((rows, d), F32), jax.ShapeDtypeStruct((rows, 3 * HYENA_WIDTH), F32),
                   jax.ShapeDtypeStruct((rows, nh), BF16), jax.ShapeDtypeStruct((rows, nh), BF16),
                   jax.ShapeDtypeStruct((rows, nh), BF16)],
        compiler_params=_params(("parallel",)), name="odd_in",
    )(x, y, pmod, mod, *consts, *tabs)


def _halo_specs(width, ntb):
    per = TM // 8

    def prev_map(i):
        return (jnp.maximum(i * per - 1, 0), 0)

    def next_map(i):
        return ((i + 1) * per - jnp.where(i % ntb == ntb - 1, 1, 0), 0)
    return pl.BlockSpec((8, width), prev_map), pl.BlockSpec((8, width), next_map)


def _halo_valid(i, ntb):
    j = i % ntb
    prev_ok = jnp.logical_and(j != 0, j != ntb - 1)
    next_ok = j < ntb - 2
    return prev_ok, next_ok


def _pool_kernel(a_ref, ap_ref, an_ref, w_ref, sc_ref, o_ref, *, ntb, seq, ctx):
    i = pl.program_id(0)
    prev_ok, next_ok = _halo_valid(i, ntb)
    a = a_ref[...]
    ap = jnp.where(prev_ok, ap_ref[...], 0.0)
    an = jnp.where(next_ok, an_ref[...], 0.0)
    ext = jnp.concatenate([ap, a, an], axis=0)
    rows_ext = TM + 16

    def shifted(d):
        return pltpu.roll(ext, (-d) % rows_ext, 0)[8:8 + TM]

    j = i % ntb
    is_ctx = j == ntb - 1
    pos = jnp.where(is_ctx, 0, j * TM) + lax.broadcasted_iota(jnp.int32, (TM, 1), 0)
    length = jnp.where(is_ctx, ctx, seq)
    lane = lax.broadcasted_iota(jnp.int32, (1, POOL_WIDTH), 1)
    acc = a
    lo, hi = 0, 1
    pooled = jnp.zeros_like(a)
    for g, w in enumerate(POOL_WINDOWS):
        for d in list(range(-w // 2, lo)) + list(range(hi, w // 2)):
            acc = acc + shifted(d)
        lo, hi = -w // 2, w // 2
        cnt = (jnp.minimum(pos + w // 2, length) - jnp.maximum(pos - w // 2, 0)).astype(F32)
        pg = acc / cnt - a
        in_group = jnp.logical_and(lane >= g * POOL_GROUP_DIM, lane < (g + 1) * POOL_GROUP_DIM)
        pooled = jnp.where(in_group, pg, pooled)
    y = _dot(pooled.astype(BF16), w_ref[...]) * sc_ref[...]
    o_ref[...] = y.astype(BF16)


def _pool(a, w_bd, scale, ntb, seq, ctx):
    rows = a.shape[0]
    prev_spec, next_spec = _halo_specs(POOL_WIDTH, ntb)
    return pl.pallas_call(
        functools.partial(_pool_kernel, ntb=ntb, seq=seq, ctx=ctx),
        grid=(rows // TM,),
        in_specs=[_tok_spec(POOL_WIDTH), prev_spec, next_spec, _full_spec(w_bd.shape), _full_spec(scale.shape)],
        out_specs=_tok_spec(POOL_WIDTH),
        out_shape=jax.ShapeDtypeStruct((rows, POOL_WIDTH), BF16),
        compiler_params=_params(("parallel",)), name="pool",
    )(a, a, a, w_bd, scale)


def _finish_heads(acc, extra_den):
    lane = lax.broadcasted_iota(jnp.int32, (1, LANES), 1)
    den = jnp.where(lane < HEAD_DIM, pltpu.roll(acc, HEAD_DIM, 1) + extra_den, 1.0)
    return acc / den


def _sink_col(sink_ref, g, nrow):
    r = lax.broadcasted_iota(jnp.int32, (SWA_GROUP * nrow, 1), 0)
    col = jnp.zeros((SWA_GROUP * nrow, 1), F32)
    for t in range(SWA_GROUP):
        col = jnp.where(r // nrow == t, sink_ref[g * SWA_GROUP + t], col)
    return col


def _swa_kernel(sink_ref, q_ref, kp_ref, kc_ref, kn_ref, kx_ref, vp_ref, vc_ref, vn_ref, vx_ref, o_ref, *, nblk):
    i = pl.program_id(1)
    nq = SWA_GROUP * SWA_BLOCK
    qrow = lax.broadcasted_iota(jnp.int32, (nq, 1), 0) % SWA_BLOCK
    kcol = lax.broadcasted_iota(jnp.int32, (1, 3 * SWA_BLOCK), 1)
    valid = jnp.logical_and(kcol >= qrow, kcol <= qrow + 2 * SWA_WINDOW)
    blk = kcol // SWA_BLOCK
    valid = jnp.logical_and(valid, jnp.logical_or(blk != 0, i > 0))
    valid = jnp.logical_and(valid, jnp.logical_or(blk != 2, i < nblk - 1))
    valid = jnp.logical_and(valid, i < nblk)
    for g in range(SWA_KV_HEADS):
        gs = slice(LANES * g, LANES * (g + 1))
        qo = LANES * SWA_GROUP * g
        q = jnp.concatenate([q_ref[:, qo + LANES * t:qo + LANES * (t + 1)] for t in range(SWA_GROUP)], axis=0)
        kb = jnp.concatenate([kp_ref[:, gs], kc_ref[:, gs], kn_ref[:, gs]], axis=0)
        vb = jnp.concatenate([vp_ref[:, gs], vc_ref[:, gs], vn_ref[:, gs]], axis=0)
        s_b = jnp.where(valid, _dot(q, kb, NT_DIMS), -jnp.inf)
        s_c = _dot(q, kx_ref[:, gs], NT_DIMS)
        sink = _sink_col(sink_ref, g, SWA_BLOCK)
        m = jnp.maximum(jnp.maximum(jnp.max(s_b, axis=-1, keepdims=True), jnp.max(s_c, axis=-1, keepdims=True)),
                        sink)
        p_b = jnp.exp(s_b - m).astype(BF16)
        p_c = jnp.exp(s_c - m).astype(BF16)
        acc = _dot(p_b, vb) + _dot(p_c, vx_ref[:, gs])
        o = _finish_heads(acc, jnp.exp(sink - m)).astype(BF16)
        for t in range(SWA_GROUP):
            o_ref[:, qo + LANES * t:qo + LANES * (t + 1)] = o[SWA_BLOCK * t:SWA_BLOCK * (t + 1)]


def _swa(q, k, v, sink, nb, seq, ctx):
    rows = q.shape[0]
    n = seq + ctx
    nblk = seq // SWA_BLOCK
    bps = n // SWA_BLOCK
    kw = SWA_KV_HEADS * LANES
    qw = SWA_Q_HEADS * LANES
    smem = pl.BlockSpec(memory_space=pltpu.SMEM)

    def kv_spec(off):
        return pl.BlockSpec((SWA_BLOCK, kw), lambda b, i: (b * bps + jnp.clip(i + off, 0, nblk - 1), 0))
    ctx_spec = pl.BlockSpec((ctx, kw), lambda b, i: (b * (n // ctx) + seq // ctx, 0))
    qo_spec = pl.BlockSpec((SWA_BLOCK, qw), lambda b, i: (b * bps + i, 0))
    return pl.pallas_call(
        functools.partial(_swa_kernel, nblk=nblk),
        grid=(nb, bps),
        in_specs=[smem, qo_spec, kv_spec(-1), kv_spec(0), kv_spec(1), ctx_spec,
                  kv_spec(-1), kv_spec(0), kv_spec(1), ctx_spec],
        out_specs=qo_spec,
        out_shape=jax.ShapeDtypeStruct((rows, qw), BF16),
        compiler_params=_params(("parallel", "parallel")), name="swa",
    )(sink, q, k, k, k, k, v, v, v, v)


MLA_TQ = 512
MLA_TK = 2048
LOG2E = math.log2(math.e)


def _mla_step(q, kc, vc, m, acc):
    s = _dot(q, kc, NT_DIMS)
    m_new = jnp.maximum(m, jnp.max(s, axis=-1, keepdims=True))
    alpha = jnp.exp2(m - m_new)
    p = jnp.exp2(s - m_new).astype(BF16)
    return m_new, alpha * acc + _dot(p, vc)


def _mla_lat_kernel(q_ref, k_ref, v_ref, o_ref, s_ref, *, seq, ctx, tk):
    q = q_ref[...]
    tq = q.shape[0]
    nchunk = seq // tk

    def scores(c):
        start = pl.multiple_of(c * tk, tk)
        return _dot(q, k_ref[pl.ds(start, tk), :], NT_DIMS)

    def half(c, slot, m, acc):
        s_ref[1 - slot] = scores(jnp.minimum(c + 1, nchunk - 1))
        s = s_ref[slot]
        start = pl.multiple_of(c * tk, tk)
        m_new = jnp.maximum(m, jnp.max(s, axis=-1, keepdims=True))
        alpha = jnp.exp2(m - m_new)
        p = jnp.exp2(s - m_new).astype(BF16)
        return m_new, alpha * acc + _dot(p, v_ref[pl.ds(start, tk), :])

    def body(c2, carry):
        m, acc = half(2 * c2, 0, *carry)
        return half(2 * c2 + 1, 1, m, acc)

    s_ref[0] = scores(0)
    init = (jnp.full((tq, 1), -jnp.inf, F32), jnp.zeros((tq, LANES), F32))
    m, acc = lax.fori_loop(0, nchunk // 2, body, init)
    _, acc = _mla_step(q, k_ref[pl.ds(seq, ctx), :], v_ref[pl.ds(seq, ctx), :], m, acc)
    o_ref[...] = _finish_heads(acc, 0.0).astype(BF16)


def _mla_ctx_kernel(q_ref, k_ref, v_ref, o_ref):
    q = q_ref[...]
    m0 = jnp.full((q.shape[0], 1), -jnp.inf, F32)
    _, acc = _mla_step(q, k_ref[...], v_ref[...], m0, jnp.zeros((q.shape[0], LANES), F32))
    o_ref[...] = _finish_heads(acc, 0.0).astype(BF16)


def _mla(q, k, v, nb, seq, ctx):
    n = seq + ctx
    q3, k3, v3 = (t.reshape(nb, n, MLA_HEADS * LANES) for t in (q, k, v))
    tq = min(MLA_TQ, seq)
    tk = min(MLA_TK, seq // 2)
    kv_spec = pl.BlockSpec((None, n, LANES), lambda b, h, i: (b, 0, h))
    qo_spec = pl.BlockSpec((None, tq, LANES), lambda b, h, i: (b, i, h))
    out = pl.pallas_call(
        functools.partial(_mla_lat_kernel, seq=seq, ctx=ctx, tk=tk),
        grid=(nb, MLA_HEADS, seq // tq),
        in_specs=[qo_spec, kv_spec, kv_spec],
        out_specs=qo_spec,
        out_shape=jax.ShapeDtypeStruct((nb, seq, MLA_HEADS * LANES), BF16),
        scratch_shapes=[pltpu.VMEM((2, tq, tk), F32)],
        compiler_params=_params(("parallel", "parallel", "parallel")), name="mla_latent",
    )(q3, k3, v3)
    cx_spec = pl.BlockSpec((None, ctx, LANES), lambda b, h: (b, seq // ctx, h))
    out_c = pl.pallas_call(
        _mla_ctx_kernel,
        grid=(nb, MLA_HEADS),
        in_specs=[cx_spec, cx_spec, cx_spec],
        out_specs=pl.BlockSpec((None, ctx, LANES), lambda b, h: (b, 0, h)),
        out_shape=jax.ShapeDtypeStruct((nb, ctx, MLA_HEADS * LANES), BF16),
        compiler_params=_params(("parallel", "parallel")), name="mla_context",
    )(q3, k3, v3)
    return out.reshape(nb * seq, MLA_HEADS * LANES), out_c.reshape(nb * ctx, MLA_HEADS * LANES)


def _hy_pre_kernel(u_ref, up_ref, un_ref, cw_ref, cb_ref, skip_ref, x0_ref, zs_ref, z_ref, *, ntb):
    i = pl.program_id(0)
    prev_ok, next_ok = _halo_valid(i, ntb)
    u = u_ref[...]
    r = lax.broadcasted_iota(jnp.int32, (TM, 1), 0)
    up_row = jnp.where(prev_ok, up_ref[7:8, :], 0.0)
    un_row = jnp.where(next_ok, un_ref[0:1, :], 0.0)
    um1 = jnp.where(r == 0, up_row, pltpu.roll(u, 1, 0))
    up1 = jnp.where(r == TM - 1, un_row, pltpu.roll(u, TM - 1, 0))
    uc = um1 * cw_ref[0:1, :] + u * cw_ref[1:2, :] + up1 * cw_ref[2:3, :] + cb_ref[...]
    w = HYENA_WIDTH
    x0 = uc[:, :w]
    z = uc[:, 2 * w:] * uc[:, w:2 * w]
    x0_ref[...] = x0
    zs_ref[...] = x0 * (skip_ref[...] * z)
    z_ref[...] = z


def _hy_pre(u, conv_w, conv_b, skip, ntb):
    rows = u.shape[0]
    w3 = 3 * HYENA_WIDTH
    prev_spec, next_spec = _halo_specs(w3, ntb)
    w = HYENA_WIDTH
    return pl.pallas_call(
        functools.partial(_hy_pre_kernel, ntb=ntb),
        grid=(rows // TM,),
        in_specs=[_tok_spec(w3), prev_spec, next_spec, _full_spec(conv_w.shape), _full_spec(conv_b.shape),
                  _full_spec(skip.shape)],
        out_specs=[_tok_spec(w), _tok_spec(w), _tok_spec(w)],
        out_shape=[jax.ShapeDtypeStruct((rows, w), F32)] * 3,
        compiler_params=_params(("parallel",)), name="hyena_pre",
    )(u, u, u, conv_w, conv_b, skip)


HIGHEST = lax.Precision.HIGHEST


def _filter_kernel(fr_ref, w1_ref, b1_ref, w2_ref, b2_ref, w3_ref, f0_ref, f1_ref, dl_ref, h_ref, n_ref,
                   *, length, tf):
    i = pl.program_id(0)
    pos = (i * tf + lax.broadcasted_iota(jnp.int32, (tf, 1), 0)).astype(F32)
    t = pos / max(length - 1, 1)
    lane = lax.broadcasted_iota(jnp.int32, (1, LANES), 1)
    bands = (HYENA_EMB - 1) // 2
    ang = (2 * math.pi / length) * pos * fr_ref[...]
    emb = jnp.where(lane == 0, t,
                    jnp.where(lane <= bands, jnp.cos(ang), jnp.where(lane <= 2 * bands, -jnp.sin(ang), 0.0)))
    h = jnp.sin(f0_ref[...] * (_dot(emb, w1_ref[...], precision=HIGHEST) + b1_ref[...]))
    h = jnp.sin(f1_ref[...] * (_dot(h, w2_ref[...], precision=HIGHEST) + b2_ref[...]))
    h = _dot(h, w3_ref[...], precision=HIGHEST)
    decay = jnp.exp(-t * dl_ref[...])
    hf = h[:, :HYENA_WIDTH] * decay
    hb = jnp.where(pos == 0.0, 0.0, h[:, HYENA_WIDTH:] * decay)
    h_ref[:, :HYENA_WIDTH] = hf
    h_ref[:, HYENA_WIDTH:] = hb

    @pl.when(i == 0)
    def _():
        n_ref[...] = jnp.zeros_like(n_ref)
    colsum = jnp.concatenate([jnp.sum(jnp.abs(hf), axis=0, keepdims=True),
                              jnp.sum(jnp.abs(hb), axis=0, keepdims=True)], axis=1)
    n_ref[...] += jnp.broadcast_to(colsum, n_ref.shape)


def _hyena_filter(length, fr, w1, b1, w2, b2, w3, f0, f1, deltas):
    tf = min(TM, length)
    consts = [fr, w1, b1, w2, b2, w3, f0, f1, deltas]
    return pl.pallas_call(
        functools.partial(_filter_kernel, length=length, tf=tf),
        grid=(length // tf,),
        in_specs=[_full_spec(c.shape) for c in consts],
        out_specs=[pl.BlockSpec((tf, 2 * HYENA_WIDTH), lambda i: (i, 0)),
                   pl.BlockSpec((8, 2 * HYENA_WIDTH), lambda i: (0, 0))],
        out_shape=[jax.ShapeDtypeStruct((length, 2 * HYENA_WIDTH), F32),
                   jax.ShapeDtypeStruct((8, 2 * HYENA_WIDTH), F32)],
        compiler_params=_params(("arbitrary",)), name="hyena_filter",
    )(*consts)


DFT_KG = 4
DFT_J = 8
DFT_WL = 512


def _dft_fwd_kernel(f_ref, x_ref, ar_ref, ai_ref):
    n1h, jj, wl = x_ref.shape
    n1 = ar_ref.shape[0]
    xs = x_ref[...].reshape(n1h * jj, wl).astype(BF16)
    ck = min(1024, n1 * jj)
    per = ck // jj
    for c in range(2 * n1 * jj // ck):
        r = _dot(f_ref[c * ck:(c + 1) * ck, :], xs).reshape(per, jj, wl)
        k0 = c * per
        if k0 < n1:
            ar_ref[k0:k0 + per] = r
        else:
            ai_ref[k0 - n1:k0 - n1 + per] = r


def _dft_fwd(f1k, x4):
    g, _, n2, w = x4.shape
    n1 = f1k.shape[0] // (2 * DFT_J)
    wl = min(DFT_WL, w)
    out_spec = pl.BlockSpec((None, n1, DFT_J, wl), lambda b, j, l: (b, 0, j, l))
    return pl.pallas_call(
        _dft_fwd_kernel,
        grid=(g, n2 // DFT_J, w // wl),
        in_specs=[pl.BlockSpec(f1k.shape, lambda b, j, l: (0, 0)),
                  pl.BlockSpec((None, n1 // 2, DFT_J, wl), lambda b, j, l: (b, 0, j, l))],
        out_specs=[out_spec, out_spec],
        out_shape=[jax.ShapeDtypeStruct((g, n1, n2, w), F32)] * 2,
        compiler_params=_params(("parallel", "parallel", "parallel")), name="hyena_dft_outer",
    )(f1k, x4)


def _spec_filter_kernel(m_ref, ar_ref, ai_ref, n_ref, gr_ref, gi_ref, *, inv_n):
    w = HYENA_WIDTH
    nrm = n_ref[0:1, :w] + n_ref[0:1, w:]
    inv = inv_n / nrm
    for kk in range(m_ref.shape[0]):
        x = jnp.concatenate([ar_ref[kk], ai_ref[kk]], axis=0).astype(BF16)
        y = _dot(m_ref[kk], x)
        n2 = y.shape[0] // 2
        gr_ref[kk] = (y[:n2, :w] + y[:n2, w:]) * inv
        gi_ref[kk] = (y[n2:, :w] - y[n2:, w:]) * inv


def _spec_kernel(m_ref, ar_ref, ai_ref, gr_ref, gi_ref, cr_ref, ci_ref):
    for kk in range(m_ref.shape[0]):
        mk = m_ref[kk]
        gr, gi = gr_ref[kk], gi_ref[kk]
        for b in range(ar_ref.shape[0]):
            x = jnp.concatenate([ar_ref[b, kk], ai_ref[b, kk]], axis=0).astype(BF16)
            y = _dot(mk, x)
            n2 = y.shape[0] // 2
            yr, yi = y[:n2], y[n2:]
            p = jnp.concatenate([yr * gr - yi * gi, yr * gi + yi * gr], axis=0).astype(BF16)
            c = _dot(mk, p, TN_DIMS)
            cr_ref[b, kk] = c[:n2]
            ci_ref[b, kk] = c[n2:]


def _spec_filter(mtab, ar, ai, nrm, n_total):
    n1 = mtab.shape[0]
    kg = min(DFT_KG, n1)
    w2 = ar.shape[-1]
    a_spec = pl.BlockSpec((kg, DFT_N2, w2), lambda j: (j, 0, 0))
    g_spec = pl.BlockSpec((kg, DFT_N2, HYENA_WIDTH), lambda j: (j, 0, 0))
    return pl.pallas_call(
        functools.partial(_spec_filter_kernel, inv_n=1.0 / n_total),
        grid=(n1 // kg,),
        in_specs=[pl.BlockSpec((kg, 2 * DFT_N2, 2 * DFT_N2), lambda j: (j, 0, 0)), a_spec, a_spec,
                  _full_spec(nrm.shape)],
        out_specs=[g_spec, g_spec],
        out_shape=[jax.ShapeDtypeStruct((n1, DFT_N2, HYENA_WIDTH), F32)] * 2,
        compiler_params=_params(("parallel",)), name="hyena_filter_spectrum",
    )(mtab, ar, ai, nrm)


def _spec(mtab, ar, ai, gr, gi):
    nb, n1 = ar.shape[0], ar.shape[1]
    kg = min(DFT_KG, n1)
    w = HYENA_WIDTH
    a_spec = pl.BlockSpec((nb, kg, DFT_N2, w), lambda j: (0, j, 0, 0))
    g_spec = pl.BlockSpec((kg, DFT_N2, w), lambda j: (j, 0, 0))
    return pl.pallas_call(
        _spec_kernel,
        grid=(n1 // kg,),
        in_specs=[pl.BlockSpec((kg, 2 * DFT_N2, 2 * DFT_N2), lambda j: (j, 0, 0)), a_spec, a_spec, g_spec, g_spec],
        out_specs=[a_spec, a_spec],
        out_shape=[jax.ShapeDtypeStruct(ar.shape, F32)] * 2,
        compiler_params=_params(("parallel",)), name="hyena_spectrum",
    )(mtab, ar, ai, gr, gi)


def _dft_inv_kernel(f_ref, cr_ref, ci_ref, x0_ref, zs_ref, o_ref):
    n1, jj, wl = cr_ref.shape
    c = jnp.concatenate([cr_ref[...].reshape(n1 * jj, wl), ci_ref[...].reshape(n1 * jj, wl)], axis=0).astype(BF16)
    y = _dot(f_ref[...], c).reshape(x0_ref.shape)
    o_ref[...] = x0_ref[...] * y + zs_ref[...]


def _dft_inv(finvk, cr, ci, x0v, zsv):
    nb, n1, n2, w = cr.shape
    c_spec = pl.BlockSpec((None, n1, DFT_J, w), lambda b, j: (b, 0, j, 0))
    t_spec = pl.BlockSpec((None, n1 // 2, DFT_J, w), lambda b, j: (b, 0, j, 0))
    return pl.pallas_call(
        _dft_inv_kernel,
        grid=(nb, n2 // DFT_J),
        in_specs=[pl.BlockSpec(finvk.shape, lambda b, j: (0, 0)), c_spec, c_spec, t_spec, t_spec],
        out_specs=t_spec,
        out_shape=jax.ShapeDtypeStruct((nb, n1 // 2, n2, w), F32),
        compiler_params=_params(("parallel", "parallel")), name="hyena_idft_outer",
    )(finvk, cr, ci, x0v, zsv)


def _hy_ctx_kernel(fc_ref, fi_ref, z_ref, h_ref, x0_ref, zs_ref, o_ref, *, n_total):
    w = HYENA_WIDTH
    h = h_ref[...]
    nk = fc_ref.shape[0] // 2
    nrm = jnp.sum(jnp.abs(h[:, :w]), axis=0, keepdims=True) + jnp.sum(jnp.abs(h[:, w:]), axis=0, keepdims=True)
    inv = (1.0 / n_total) / nrm
    hs = _dot(fc_ref[...], h.astype(BF16))
    gr = (hs[:nk, :w] + hs[:nk, w:]) * inv
    gi = (hs[nk:, :w] - hs[nk:, w:]) * inv
    zsp = _dot(fc_ref[...], z_ref[...].astype(BF16))
    zr, zi = zsp[:nk], zsp[nk:]
    p = jnp.concatenate([zr * gr - zi * gi, zr * gi + zi * gr], axis=0).astype(BF16)
    y = _dot(fi_ref[...], p)
    o_ref[...] = x0_ref[...] * y + zs_ref[...]


def _hy_ctx(fc, fi, z, hcat, x0, zs, nb, seq, ctx):
    n = seq + ctx
    w = HYENA_WIDTH
    row_spec = pl.BlockSpec((ctx, w), lambda b: (b * (n // ctx) + seq // ctx, 0))
    return pl.pallas_call(
        functools.partial(_hy_ctx_kernel, n_total=2 * ctx),
        grid=(nb,),
        in_specs=[_full_spec(fc.shape), _full_spec(fi.shape), row_spec, _full_spec(hcat.shape), row_spec, row_spec],
        out_specs=pl.BlockSpec((ctx, w), lambda b: (b, 0)),
        out_shape=jax.ShapeDtypeStruct((nb * ctx, w), F32),
        compiler_params=_params(("parallel",)), name="hyena_context",
    )(fc, fi, z, hcat, x0, zs)


def _dft_tables(seq, ctx):
    n = 2 * seq
    n1 = n // DFT_N2
    two_pi = 2.0 * math.pi

    def cs(num, den):
        ang = two_pi * (num % den).astype(F32) / den
        return jnp.cos(ang), jnp.sin(ang)
    k1 = jnp.arange(n1, dtype=jnp.int32)
    big = jnp.arange(2 * n1 * DFT_J, dtype=jnp.int32)
    small = jnp.arange(n1 // 2 * DFT_J, dtype=jnp.int32)
    kk, jb = big // DFT_J, big % DFT_J
    mm, js = small // DFT_J, small % DFT_J
    c, s = cs((kk % n1)[:, None] * mm[None, :], n1)
    f1k = jnp.where(jb[:, None] == js[None, :], jnp.where(kk[:, None] < n1, c, -s), 0.0).astype(BF16)
    finvk = f1k.T
    k2 = jnp.arange(DFT_N2, dtype=jnp.int32)
    phase = k2[None, None, :] * (k1[:, None, None] + n1 * k2[None, :, None])
    c, s = cs(phase, n)
    mtab = jnp.concatenate([jnp.concatenate([c, s], axis=2), jnp.concatenate([-s, c], axis=2)], axis=1)
    nc = 2 * ctx
    kk = jnp.arange(nc, dtype=jnp.int32)
    c, s = cs(kk[:, None] * kk[None, :ctx], nc)
    fc = jnp.concatenate([c, -s], axis=0).astype(BF16)
    fi = jnp.concatenate([c.T, -s.T], axis=1).astype(BF16)
    return f1k, finvk, mtab.astype(BF16), fc, fi


N_CLASSES = N_GROUPS * 6
CLS_ROWS = 32
PAIR_LO = (0, 0, 0, 1, 1, 2)
PAIR_HI = (1, 2, 3, 2, 3, 3)
HEXT = LANES


def _post_kernel(*refs, ntb, split):
    if split:
        (a1_ref, a1c_ref, a2_ref, a2c_ref, w1_ref, w2_ref, x_ref, mod_ref, g_ref, rw_ref, rb_ref,
         xo_ref, hx_ref, meta_ref, cnt_ref, carry_ref) = refs
        is_ctx = pl.program_id(0) % ntb == ntb - 1
        a1 = jnp.where(is_ctx, a1c_ref[...], a1_ref[...])
        a2 = jnp.where(is_ctx, a2c_ref[...], a2_ref[...])
    else:
        (a1_ref, a2_ref, w1_ref, w2_ref, x_ref, mod_ref, g_ref, rw_ref, rb_ref,
         xo_ref, hx_ref, meta_ref, cnt_ref, carry_ref) = refs
        a1, a2 = a1_ref[...], a2_ref[...]
    step = pl.program_id(0)

    @pl.when(step == 0)
    def _():
        carry_ref[...] = jnp.zeros_like(carry_ref)
    ml = _dot(a1.astype(BF16), w1_ref[...]) + _dot(a2, w2_ref[...])
    x = x_ref[...] + mod_ref[2:3, :] * ml
    xo_ref[...] = x
    hf = _modulate(x, g_ref[...], mod_ref[3:4, :], mod_ref[4:5, :])
    d = hf.shape[1]
    hx_ref[:, :d] = hf
    h = hf.astype(BF16)
    scores = jax.nn.sigmoid(_dot(rw_ref[...], h, NT_DIMS))
    biased = scores + rb_ref[...]
    sc = [scores[e:e + 1, :] for e in range(N_EXPERTS)]
    bi = [biased[e:e + 1, :] for e in range(N_EXPERTS)]
    best = None
    sel = None
    for g in range(N_GROUPS):
        v0, v1, v2, v3 = bi[4 * g:4 * g + 4]
        top1 = jnp.maximum(jnp.maximum(v0, v1), jnp.maximum(v2, v3))
        top2 = jnp.maximum(jnp.maximum(jnp.minimum(v0, v1), jnp.minimum(v2, v3)),
                           jnp.minimum(jnp.maximum(v0, v1), jnp.maximum(v2, v3)))
        gs = top1 + top2
        if g == 0:
            best, sel = gs, jnp.zeros_like(gs, dtype=jnp.int32)
        else:
            upd = gs > best
            best = jnp.where(upd, gs, best)
            sel = jnp.where(upd, g, sel)

    def pick(vals, j):
        out = vals[j]
        for g in range(1, N_GROUPS):
            out = jnp.where(sel == g, vals[4 * g + j], out)
        return out
    b = [pick(bi, j) for j in range(EXPERTS_PER_GROUP)]
    s = [pick(sc, j) for j in range(EXPERTS_PER_GROUP)]
    i1 = jnp.zeros_like(sel)
    m1 = b[0]
    for j in range(1, EXPERTS_PER_GROUP):
        upd = b[j] > m1
        m1 = jnp.where(upd, b[j], m1)
        i1 = jnp.where(upd, j, i1)
    i2 = jnp.full_like(sel, -1)
    m2 = jnp.full_like(m1, -jnp.inf)
    for j in range(EXPERTS_PER_GROUP):
        upd = jnp.logical_and(i1 != j, b[j] > m2)
        m2 = jnp.where(upd, b[j], m2)
        i2 = jnp.where(upd, j, i2)
    w1 = s[0]
    w2 = s[0]
    for j in range(1, EXPERTS_PER_GROUP):
        w1 = jnp.where(i1 == j, s[j], w1)
        w2 = jnp.where(i2 == j, s[j], w2)
    tot = w1 + w2
    first_lo = i1 < i2
    lo = jnp.minimum(i1, i2)
    hi = jnp.maximum(i1, i2)
    cls = sel * 6 + jnp.where(lo == 0, 0, jnp.where(lo == 1, 3, 5)) + hi - lo - 1
    w_lo = jnp.where(first_lo, w1, w2) / tot
    w_hi = jnp.where(first_lo, w2, w1) / tot
    lrow = lax.broadcasted_iota(jnp.int32, (LANES, 1), 0)
    wt = jnp.where(lrow == 0, w_lo, jnp.where(lrow == 1, w_hi, 0.0))
    hx_ref[:, d:] = wt.T
    tm = cls.shape[1]
    crow = lax.broadcasted_iota(jnp.int32, (CLS_ROWS, 1), 0)
    onehot = crow == cls
    tri = (lax.broadcasted_iota(jnp.int32, (tm, tm), 0) <= lax.broadcasted_iota(jnp.int32, (tm, tm), 1))
    cum = _dot(onehot.astype(BF16), tri.astype(BF16))
    carry = carry_ref[...]
    rank = jnp.sum(jnp.where(onehot, carry[:, 0:1] + cum, 0.0), axis=0, keepdims=True) - 1.0
    mrow = lax.broadcasted_iota(jnp.int32, (8, 1), 0)
    meta_ref[...] = jnp.where(mrow == 0, cls, jnp.where(mrow == 1, rank.astype(jnp.int32), 0))
    carry = carry + cum[:, tm - 1:tm]
    carry_ref[...] = carry
    cnt_ref[...] = carry


def _post(a1, a2, w1, w2, x, mod, g, rw, rb, ntb):
    rows, d = x.shape
    row = _mod_row(ntb)
    mod_spec = pl.BlockSpec((None, 8, d), lambda i: (row(i), 0, 0))
    nt = rows // TM
    split = isinstance(a1, tuple)
    if split:
        nlt = ntb - 1

        def lat(width):
            return pl.BlockSpec((TM, width), lambda i: ((i // ntb) * nlt + jnp.minimum(i % ntb, nlt - 1), 0))

        def cx(width):
            return pl.BlockSpec((TM, width), lambda i: (i // ntb, 0))
        mix = [a1[0], a1[1], a2[0], a2[1]]
        mix_specs = [lat(a1[0].shape[1]), cx(a1[1].shape[1]), lat(a2[0].shape[1]), cx(a2[1].shape[1])]
    else:
        mix = [a1, a2]
        mix_specs = [_tok_spec(a1.shape[1]), _tok_spec(a2.shape[1])]
    return pl.pallas_call(
        functools.partial(_post_kernel, ntb=ntb, split=split),
        grid=(nt,),
        in_specs=mix_specs + [_full_spec(w1.shape), _full_spec(w2.shape),
                              _tok_spec(d), mod_spec, _full_spec(g.shape), _full_spec(rw.shape), _full_spec(rb.shape)],
        out_specs=[_tok_spec(d), _tok_spec(d + HEXT), pl.BlockSpec((None, 8, TM), lambda i: (i, 0, 0)),
                   pl.BlockSpec((CLS_ROWS, LANES), lambda i: (0, 0))],
        out_shape=[jax.ShapeDtypeStruct((rows, d), F32), jax.ShapeDtypeStruct((rows, d + HEXT), F32),
                   jax.ShapeDtypeStruct((nt, 8, TM), jnp.int32), jax.ShapeDtypeStruct((CLS_ROWS, LANES), F32)],
        scratch_shapes=[pltpu.VMEM((CLS_ROWS, LANES), F32)],
        compiler_params=_params(("arbitrary",)), name="post_mixer",
    )(*mix, w1, w2, x, mod, g, rw, rb)


MOE_TM = 256
MOE_TD = 512
MOE_UNROLL = 8


def _route_plan(meta, cnt, rows):
    counts = cnt[:N_CLASSES, 0].astype(jnp.int32)
    padded = ((counts + MOE_TM - 1) // MOE_TM) * MOE_TM
    ends = jnp.cumsum(padded)
    offs = ends - padded
    cls = meta[:, 0, :].reshape(rows)
    rank = meta[:, 1, :].reshape(rows)
    slot = offs[cls] + rank
    ntiles = rows // MOE_TM + N_CLASSES
    starts = jnp.arange(ntiles, dtype=jnp.int32) * MOE_TM
    tcls = jnp.sum((ends[None, :] <= starts[:, None]).astype(jnp.int32), axis=1)
    valid = tcls < N_CLASSES
    nvalid = jnp.sum(valid.astype(jnp.int32))
    last = tcls[jnp.maximum(nvalid - 1, 0)]
    tcls = jnp.where(valid, tcls, jnp.minimum(last, N_CLASSES - 1))
    grp, pair = tcls // 6, tcls % 6
    ea = grp * EXPERTS_PER_GROUP + jnp.asarray(PAIR_LO, jnp.int32)[pair]
    eb = grp * EXPERTS_PER_GROUP + jnp.asarray(PAIR_HI, jnp.int32)[pair]
    return slot, ea, eb, valid.astype(jnp.int32), ntiles


def _row_copy(src, dst, sem, s_row, d_row):
    return pltpu.make_async_copy(src.at[pl.ds(s_row, 1), :], dst.at[pl.ds(d_row, 1), :], sem)


def _dispatch_kernel(slot_ref, hx_ref, init_ref, xs_ref, sem):
    del init_ref
    td = hx_ref.shape[0]

    def issue(t, c):
        _row_copy(hx_ref, xs_ref, sem, t, slot_ref[0, t]).start()
        return c
    lax.fori_loop(0, td, issue, 0, unroll=MOE_UNROLL)
    pltpu.make_async_copy(hx_ref, xs_ref.at[pl.ds(0, td), :], sem).wait()


def _dispatch(hx, slot, nrows_sorted):
    rows, width = hx.shape
    nsteps = rows // MOE_TD
    init = jnp.zeros((nrows_sorted, width), F32)
    return pl.pallas_call(
        _dispatch_kernel,
        grid=(nsteps,),
        in_specs=[pl.BlockSpec((None, 1, MOE_TD), lambda i: (i, 0, 0), memory_space=pltpu.SMEM),
                  pl.BlockSpec((MOE_TD, width), lambda i: (i, 0)),
                  pl.BlockSpec(memory_space=pl.ANY)],
        out_specs=pl.BlockSpec(memory_space=pl.ANY),
        out_shape=jax.ShapeDtypeStruct((nrows_sorted, width), F32),
        scratch_shapes=[pltpu.SemaphoreType.DMA(())],
        input_output_aliases={2: 0},
        compiler_params=_params(("arbitrary",)), name="moe_dispatch",
    )(slot.reshape(nsteps, 1, MOE_TD), hx, init)


def _experts_kernel(ea_ref, eb_ref, valid_ref, xs_ref, wga_ref, wua_ref, wda_ref, wgb_ref, wub_ref, wdb_ref, ys_ref):
    del ea_ref, eb_ref
    j = pl.program_id(0)

    @pl.when(valid_ref[j] == 0)
    def _():
        ys_ref[...] = jnp.zeros_like(ys_ref)

    @pl.when(valid_ref[j] != 0)
    def _():
        d = ys_ref.shape[1]
        x = xs_ref[:, :d].astype(BF16)

        def ffn(wg_ref, wu_ref, wd_ref, w):
            gate = _dot(x, wg_ref[...])
            up = _dot(x, wu_ref[...])
            a = (gate * jax.nn.sigmoid(gate) * up * w).astype(BF16)
            return _dot(a, wd_ref[...])
        ys_ref[...] = (ffn(wga_ref, wua_ref, wda_ref, xs_ref[:, d:d + 1])
                       + ffn(wgb_ref, wub_ref, wdb_ref, xs_ref[:, d + 1:d + 2]))


def _experts(xs, ea, eb, valid, wg, wu, wd, layer, ntiles):
    d = wg.shape[2]
    ff = wg.shape[3]
    gu_a = pl.BlockSpec((None, None, d, ff), lambda j, ea, eb, v: (layer, ea[j], 0, 0))
    gu_b = pl.BlockSpec((None, None, d, ff), lambda j, ea, eb, v: (layer, eb[j], 0, 0))
    dn_a = pl.BlockSpec((None, None, ff, d), lambda j, ea, eb, v: (layer, ea[j], 0, 0))
    dn_b = pl.BlockSpec((None, None, ff, d), lambda j, ea, eb, v: (layer, eb[j], 0, 0))
    return pl.pallas_call(
        _experts_kernel,
        grid_spec=pltpu.PrefetchScalarGridSpec(
            num_scalar_prefetch=3, grid=(ntiles,),
            in_specs=[pl.BlockSpec((MOE_TM, xs.shape[1]), lambda j, ea, eb, v: (j, 0)),
                      gu_a, gu_a, dn_a, gu_b, gu_b, dn_b],
            out_specs=pl.BlockSpec((MOE_TM, d), lambda j, ea, eb, v: (j, 0))),
        out_shape=jax.ShapeDtypeStruct((ntiles * MOE_TM, d), F32),
        compiler_params=_params(("arbitrary",)), name="moe_experts",
    )(ea, eb, valid, xs, wg, wu, wd, wg, wu, wd)


def _undispatch_kernel(slot_ref, ys_ref, y_ref, sem):
    td = y_ref.shape[0]

    def issue(t, c):
        _row_copy(ys_ref, y_ref, sem, slot_ref[0, t], t).start()
        return c
    lax.fori_loop(0, td, issue, 0, unroll=MOE_UNROLL)
    pltpu.make_async_copy(ys_ref.at[pl.ds(0, td), :], y_ref, sem).wait()


def _undispatch(ys, slot, rows):
    d = ys.shape[1]
    nsteps = rows // MOE_TD
    return pl.pallas_call(
        _undispatch_kernel,
        grid=(nsteps,),
        in_specs=[pl.BlockSpec((None, 1, MOE_TD), lambda i: (i, 0, 0), memory_space=pltpu.SMEM),
                  pl.BlockSpec(memory_space=pl.ANY)],
        out_specs=pl.BlockSpec((MOE_TD, d), lambda i: (i, 0)),
        out_shape=jax.ShapeDtypeStruct((rows, d), F32),
        scratch_shapes=[pltpu.SemaphoreType.DMA(())],
        compiler_params=_params(("arbitrary",)), name="moe_undispatch",
    )(slot.reshape(nsteps, 1, MOE_TD), ys)


def _moe(hx, meta, cnt, wg, wu, wd, layer):
    rows = hx.shape[0]
    slot, ea, eb, valid, ntiles = _route_plan(meta, cnt, rows)
    xs = _dispatch(hx, slot, ntiles * MOE_TM)
    ys = _experts(xs, ea, eb, valid, wg, wu, wd, layer, ntiles)
    return _undispatch(ys, slot, rows)


def _final_kernel(x_ref, y_ref, mod_ref, o_ref):
    o_ref[...] = x_ref[...] + mod_ref[5:6, :] * y_ref[...]


def _final(x, y, mod, nb, seq, ctx):
    d = x.shape[1]
    ntb = (seq + ctx) // TM
    nlt = seq // TM
    tok = pl.BlockSpec((TM, d), lambda b, j: (b * ntb + j, 0))
    return pl.pallas_call(
        _final_kernel,
        grid=(nb, nlt),
        in_specs=[tok, tok, pl.BlockSpec((None, 8, d), lambda b, j: (b, 0, 0))],
        out_specs=pl.BlockSpec((None, TM, d), lambda b, j: (b, j, 0)),
        out_shape=jax.ShapeDtypeStruct((nb, seq, d), F32),
        compiler_params=_params(("parallel", "parallel")), name="final_residual",
    )(x, y, mod)


def _pad_heads(w, heads, dim, axis):
    shp = w.shape
    w = w.reshape(shp[:axis] + (heads, dim) + shp[axis + 1:])
    pad = [(0, 0)] * w.ndim
    pad[axis + 1] = (0, LANES - dim)
    w = jnp.pad(w, pad)
    return w.reshape(shp[:axis] + (heads * LANES,) + shp[axis + 1:])


def _pad_vec(v, mult=1.0):
    return jnp.pad(v.astype(F32) * mult, (0, LANES - v.shape[0])).reshape(1, LANES)


def _rope_tables(seq, ctx, d_rot, off):
    rows = seq // GRID_W
    row = jnp.repeat(jnp.arange(rows), GRID_W).astype(F32)
    col = jnp.tile(jnp.arange(GRID_W), rows).astype(F32)
    n_freq = d_rot // 4
    inv = ROPE_BASE ** (-jnp.arange(n_freq, dtype=F32) / n_freq)
    ang = jnp.concatenate([row[:, None] * inv, col[:, None] * inv], axis=-1)
    cos, sin = jnp.cos(ang), jnp.sin(ang)
    half = d_rot // 2

    def z(r, w):
        return jnp.zeros((r, w), F32)
    rest = LANES - off - 2 * half
    cos_l = jnp.concatenate([jnp.ones((seq, off), F32), cos, cos, z(seq, rest)], axis=1)
    sa_l = jnp.concatenate([z(seq, off + half), sin, z(seq, rest)], axis=1)
    sb_l = jnp.concatenate([z(seq, off), -sin, z(seq, half + rest)], axis=1)
    cos_c = jnp.concatenate([jnp.ones((ctx, off + 2 * half), F32), z(ctx, rest)], axis=1)
    return (jnp.concatenate([cos_l, cos_c], axis=0), jnp.concatenate([sa_l, z(ctx, LANES)], axis=0),
            jnp.concatenate([sb_l, z(ctx, LANES)], axis=0))


def _rot_cols(w):
    half = MLA_ROPE // 2
    w3 = w.reshape(w.shape[0], MLA_HEADS, LANES)
    z = jnp.zeros_like(w3)
    rot = jnp.concatenate([z[..., :MLA_NOPE], -w3[..., MLA_NOPE + half:MLA_NOPE + 2 * half],
                           w3[..., MLA_NOPE:MLA_NOPE + half], z[..., MLA_NOPE + 2 * half:]], axis=-1)
    return rot.reshape(w.shape)


def _gain_tables(tabs, gain):
    cos_t, sa_t, sb_t = tabs
    half = MLA_ROPE // 2
    return (cos_t * gain, sa_t * jnp.roll(gain, half, axis=1) - sb_t * jnp.roll(gain, LANES - half, axis=1))


def _forward(x, c, ctx, c_ctx, ada_w, ada_b, norm_g, ev_w_in, ev_w_out, pool_w, pool_scale,
             swa_q_gain, swa_k_gain, swa_sink, od_w_in, od_w_out, hy_conv_w, hy_conv_b,
             hy_w1, hy_b1, hy_w2, hy_b2, hy_w3, hy_freq, hy_skip, mla_cq_gain, mla_ckv_gain,
             mla_w_uq, mla_w_ukv, mla_q_gain, mla_k_gain, router_w, router_b,
             moe_w_gate, moe_w_up, moe_w_down):
    nb, seq, d = x.shape
    nctx = ctx.shape[1]
    depth = ada_w.shape[0]
    assert nctx == TM and seq % (2 * TM) == 0 and seq % GRID_W == 0 and nb <= 2
    n = seq + nctx
    ntb = n // TM
    rows = nb * n

    cvec = jnp.concatenate([c, c_ctx[None, :], jnp.zeros((8 - nb - 1, d), F32)], axis=0)
    if nb == 1:
        cvec = jnp.concatenate([c, jnp.zeros((1, d), F32), c_ctx[None, :], jnp.zeros((5, d), F32)], axis=0)
    mod = _adaln(cvec, ada_w, ada_b)

    xs = jnp.concatenate([x, ctx], axis=1).reshape(rows, d)
    tabs_swa = _rope_tables(seq, nctx, HEAD_DIM, 0)
    tabs_mla = _rope_tables(seq, nctx, MLA_ROPE, MLA_NOPE)
    f1k, finvk, mtab, fc, fi = _dft_tables(seq, nctx)
    bands = (HYENA_EMB - 1) // 2
    frv = jnp.linspace(1e-4, bands - 1, bands, dtype=F32)
    fr = jnp.concatenate([jnp.zeros((1,), F32), frv, frv, jnp.zeros((LANES - 1 - 2 * bands,), F32)]).reshape(1, LANES)
    deltas = jnp.abs(jnp.linspace(math.log(HYENA_TARGET) / HYENA_FAST_DECAY,
                                  math.log(HYENA_TARGET) / HYENA_SLOW_DECAY, HYENA_WIDTH, dtype=F32)).reshape(1, -1)
    rw = jnp.transpose(router_w).astype(BF16)
    rb = router_b.astype(F32).reshape(N_EXPERTS, 1)

    wg16, wu16, wd16 = moe_w_gate.astype(BF16), moe_w_up.astype(BF16), moe_w_down.astype(BF16)

    prev = None
    y = None
    for layer in range(depth):
        i = layer // 2
        lmod = mod[layer]
        g1 = norm_g[layer, 0].reshape(1, d)
        g2 = norm_g[layer, 1].reshape(1, d)
        if layer % 2 == 0:
            w = ev_w_in[i]
            o1 = POOL_WIDTH
            o2 = o1 + SWA_Q_HEADS * HEAD_DIM
            o3 = o2 + SWA_KV_HEADS * HEAD_DIM
            w_in = jnp.concatenate([w[:, :o1], _pad_heads(w[:, o1:o2], SWA_Q_HEADS, HEAD_DIM, 1),
                                    _pad_heads(w[:, o2:o3], SWA_KV_HEADS, HEAD_DIM, 1),
                                    _pad_heads(w[:, o3:], SWA_KV_HEADS, HEAD_DIM, 1)], axis=1).astype(BF16)
            qg = _pad_vec(swa_q_gain[i], HEAD_DIM ** -0.5)
            kg = _pad_vec(swa_k_gain[i])
            xs, a, q, k, v = _even_in(xs, prev, lmod, g1, w_in, qg, kg, tabs_swa, ntb)
            w_bd = jnp.zeros((POOL_WIDTH, POOL_WIDTH), F32)
            for g in range(POOL_GROUPS):
                sl = slice(g * POOL_GROUP_DIM, (g + 1) * POOL_GROUP_DIM)
                w_bd = w_bd.at[sl, sl].set(pool_w[i, g])
            mix1 = _pool(a, w_bd.astype(BF16), pool_scale[i].reshape(1, -1), ntb, seq, nctx)
            mix2 = _swa(q, k, v, swa_sink[i].astype(F32), nb, seq, nctx)
            wo = ev_w_out[i]
            wo1 = wo[:POOL_WIDTH].astype(BF16)
            wo2 = _pad_heads(wo[POOL_WIDTH:], SWA_Q_HEADS, HEAD_DIM, 0).astype(BF16)
        else:
            w_in = jnp.pad(od_w_in[i], ((0, 0), (0, LANES - MLA_ROPE))).astype(BF16)
            wuq = _pad_heads(mla_w_uq[i], MLA_HEADS, MLA_QK, 1)
            wuq = jnp.concatenate([wuq, _rot_cols(wuq)], axis=1).astype(BF16)
            wukv = mla_w_ukv[i].reshape(MLA_KV_RANK, MLA_HEADS, MLA_NOPE + MLA_V)
            wuk = _pad_heads(wukv[:, :, :MLA_NOPE].reshape(MLA_KV_RANK, -1), MLA_HEADS, MLA_NOPE, 1).astype(BF16)
            wuv = _pad_heads(wukv[:, :, MLA_NOPE:].reshape(MLA_KV_RANK, -1), MLA_HEADS, MLA_V, 1).astype(BF16)
            qg = _pad_vec(mla_q_gain[i], MLA_QK ** -0.5 * LOG2E)
            kg = _pad_vec(mla_k_gain[i])
            xs, u, q, k, v = _odd_in(xs, prev, lmod, g1, w_in, mla_cq_gain[i].reshape(1, -1),
                                     mla_ckv_gain[i].reshape(1, -1), wuq, wuk, wuv,
                                     _gain_tables(tabs_mla, qg) + _gain_tables(tabs_mla, kg), ntb)
            x0, zs, z = _hy_pre(u, hy_conv_w[i], hy_conv_b[i].reshape(1, -1), hy_skip[i].reshape(1, -1), ntb)
            w1p = jnp.zeros((LANES, LANES), F32).at[:HYENA_EMB, :HYENA_HIDDEN].set(hy_w1[i])
            w2p = jnp.zeros((LANES, LANES), F32).at[:HYENA_HIDDEN, :HYENA_HIDDEN].set(hy_w2[i])
            w3p = jnp.zeros((LANES, 2 * HYENA_WIDTH), F32).at[:HYENA_HIDDEN].set(hy_w3[i])
            fparams = (fr, w1p, _pad_vec(hy_b1[i]), w2p, _pad_vec(hy_b2[i]), w3p,
                       _pad_vec(hy_freq[i, 0]), _pad_vec(hy_freq[i, 1]), deltas)
            hcat, nrm = _hyena_filter(seq, *fparams)
            hcat_c, _ = _hyena_filter(nctx, *fparams)
            n1 = 2 * seq // DFT_N2
            w = HYENA_WIDTH
            har, hai = _dft_fwd(f1k, hcat.reshape(1, n1 // 2, DFT_N2, 2 * w))
            gr, gi = _spec_filter(mtab, har[0], hai[0], nrm, 2 * seq)
            view = (nb, n // DFT_N2, DFT_N2, w)
            ar, ai = _dft_fwd(f1k, z.reshape(view))
            cr, ci = _spec(mtab, ar, ai, gr, gi)
            hy = _dft_inv(finvk, cr, ci, x0.reshape(view), zs.reshape(view))
            mix1 = (hy.reshape(nb * seq, w), _hy_ctx(fc, fi, z, hcat_c, x0, zs, nb, seq, nctx))
            mix2 = _mla(q, k, v, nb, seq, nctx)
            wo = od_w_out[i]
            wo1 = wo[:HYENA_WIDTH].astype(BF16)
            wo2 = _pad_heads(wo[HYENA_WIDTH:], MLA_HEADS, MLA_V, 0).astype(BF16)
        xs, hx, meta, cnt = _post(mix1, mix2, wo1, wo2, xs, lmod, g2, rw, rb, ntb)
        y = _moe(hx, meta, cnt, wg16, wu16, wd16, layer)
        prev = (y, lmod)
    return _final(xs, y, mod[depth - 1], nb, seq, nctx)


def kernel(x, c, ctx, c_ctx, ada_w, ada_b, norm_g, ev_w_in, ev_w_out, pool_w, pool_scale, swa_q_gain, swa_k_gain, swa_sink, od_w_in, od_w_out, hy_conv_w, hy_conv_b, hy_w1, hy_b1, hy_w2, hy_b2, hy_w3, hy_freq, hy_skip, mla_cq_gain, mla_ckv_gain, mla_w_uq, mla_w_ukv, mla_q_gain, mla_k_gain, router_w, router_b, moe_w_gate, moe_w_up, moe_w_down):
    return _forward(x, c, ctx, c_ctx, ada_w, ada_b, norm_g, ev_w_in, ev_w_out, pool_w, pool_scale,
                    swa_q_gain, swa_k_gain, swa_sink, od_w_in, od_w_out, hy_conv_w, hy_conv_b,
                    hy_w1, hy_b1, hy_w2, hy_b2, hy_w3, hy_freq, hy_skip, mla_cq_gain, mla_ckv_gain,
                    mla_w_uq, mla_w_ukv, mla_q_gain, mla_k_gain, router_w, router_b,
                    moe_w_gate, moe_w_up, moe_w_down)
```
